```python
import math
import jax, jax.numpy as jnp
from jax import lax
import numpy as np

D_MODEL = 2048
BATCH = 4
SEQ = 8192
DEPTH = 2
DEC_BATCH = 16
DEC_SEQ = 64
PAST_LEN = 2048

CHUNK = 64
N_MIXERS = 2
N_ATTN_LAYERS = (DEPTH + 1) // 2
N_CONV_LAYERS = DEPTH // 2
N_HEADS = 16
HEAD_DIM = D_MODEL // N_HEADS
N_KV_HEADS = 4
GROUP = N_HEADS // N_KV_HEADS
N_IDX_HEADS = 16
IDX_DIM = 64
IDX_SCALE = (N_IDX_HEADS ** -0.5) * (IDX_DIM ** -0.5)
TOPK_MAX = 256
Q_BLOCK = 128
N_BUCKETS = 32
MAX_DISTANCE = 128
CONV_DIM = D_MODEL
DW_WIDTH = 31
CONV_STATE = DW_WIDTH - 1
D_FF = ((8 * D_MODEL + 3 * 256 - 1) // (3 * 256)) * 256
RMS_EPS = 1e-6
LN_EPS = 1e-5

kernel_name = 'hybrid_dsa_conformer_stream_step'


def _rmsnorm(x, g):
    xf = x.astype(jnp.float32)
    y = xf * lax.rsqrt(jnp.mean(xf * xf, axis=-1, keepdims=True) + RMS_EPS)
    return (y * g.astype(jnp.float32)).astype(x.dtype)


def _t5_bucket(rel):
    half = N_BUCKETS // 2
    max_exact = half // 2
    n = jnp.abs(rel)
    nf = jnp.maximum(n, 1).astype(jnp.float32)
    large = max_exact + (jnp.log(nf / max_exact) / math.log(MAX_DISTANCE / max_exact)
                         * (half - max_exact)).astype(jnp.int32)
    large = jnp.minimum(large, half - 1)
    return jnp.where(rel > 0, half, 0) + jnp.where(n < max_exact, n, large)


def _dsa_block(q, q_idx, w_idx, q_pos, k_all, v_all, kidx_all, rel_bias, n_keep):
    b, tq = q.shape[0], q.shape[1]
    n_keys = kidx_all.shape[1]
    dots = jnp.einsum('bqhd,bsd->bqhs', q_idx, kidx_all)
    score = jnp.einsum('bqh,bqhs->bqs', w_idx, jax.nn.relu(dots)).astype(jnp.float32)
    q_chunk = q_pos // CHUNK
    admissible = (jnp.arange(n_keys) // CHUNK)[None, :] <= q_chunk[:, None]
    score = jnp.where(admissible[None], score, -jnp.inf)
    _, sel = lax.top_k(score, n_keep)
    gather = jax.vmap(lambda rows, ids: rows[ids])
    k_sel = gather(k_all, sel)
    v_sel = gather(v_all, sel)
    valid = (sel // CHUNK) <= q_chunk[None, :, None]
    bias = rel_bias[_t5_bucket(sel - q_pos[None, :, None])]
    bias = bias.reshape(b, tq, n_keep, N_KV_HEADS, GROUP).transpose(0, 1, 3, 4, 2)
    qg = q.reshape(b, tq, N_KV_HEADS, GROUP, HEAD_DIM)
    logits = jnp.einsum('bqhgd,bqnhd->bqhgn', qg, k_sel).astype(jnp.float32) * (HEAD_DIM ** -0.5)
    logits = jnp.where(valid[:, :, None, None, :], logits + bias.astype(jnp.float32), -jnp.inf)
    p = jax.nn.softmax(logits, axis=-1).astype(v_sel.dtype)
    out = jnp.einsum('bqhgn,bqnhd->bqhgd', p, v_sel)
    return out.reshape(b, tq, N_HEADS * HEAD_DIM)


def _dsa_mixer(h, past, past_len, rel_bias, wq, wk, wv, wo, idx_wq, idx_wk, idx_ww):
    b, t, _ = h.shape
    q = (h @ wq).reshape(b, t, N_HEADS, HEAD_DIM)
    k = (h @ wk).reshape(b, t, N_KV_HEADS, HEAD_DIM)
    v = (h @ wv).reshape(b, t, N_KV_HEADS, HEAD_DIM)
    qi = (h @ idx_wq).reshape(b, t, N_IDX_HEADS, IDX_DIM)
    ki = h @ idx_wk
    wi = (h @ idx_ww) * IDX_SCALE
    if past is None:
        k_all, v_all, ki_all = k, v, ki
    else:
        k_past, v_past, ki_past = past
        k_all = jnp.concatenate([k_past.astype(k.dtype), k], axis=1)
        v_all = jnp.concatenate([v_past.astype(v.dtype), v], axis=1)
        ki_all = jnp.concatenate([ki_past.astype(ki.dtype), ki], axis=1)
    n_keep = min(TOPK_MAX, (past_len + t) // 4)
    q_pos = past_len + jnp.arange(t, dtype=jnp.int32)
    nb = t // Q_BLOCK if t % Q_BLOCK == 0 else 1
    blk = t // nb

    def to_blocks(a):
        return jnp.moveaxis(a.reshape((b, nb, blk) + a.shape[2:]), 1, 0)

    out = lax.map(
        lambda xs: _dsa_block(xs[0], xs[1], xs[2], xs[3], k_all, v_all, ki_all, rel_bias, n_keep),
        (to_blocks(q), to_blocks(qi), to_blocks(wi), q_pos.reshape(nb, blk)))
    out = jnp.moveaxis(out, 0, 1).reshape(b, t, N_HEADS * HEAD_DIM)
    return out @ wo, k, v, ki


def _conv_mixer(h, past, w_pw1, b_pw1, w_dw, b_dw, ln_g, ln_b, w_pw2, b_pw2):
    b, t, _ = h.shape
    a, gate = jnp.split(h @ w_pw1 + b_pw1, 2, axis=-1)
    u = a * jax.nn.sigmoid(gate)
    if past is None:
        past = jnp.zeros((b, CONV_STATE, CONV_DIM), u.dtype)
    u_ext = jnp.concatenate([past.astype(u.dtype), u], axis=1)
    dw = lax.conv_general_dilated(u_ext, w_dw[:, None, :].astype(u.dtype), window_strides=(1,),
                                  padding='VALID', dimension_numbers=('NWC', 'WIO', 'NWC'),
                                  feature_group_count=CONV_DIM) + b_dw
    df = dw.astype(jnp.float32)
    mu = jnp.mean(df, axis=-1, keepdims=True)
    var = jnp.mean(jnp.square(df - mu), axis=-1, keepdims=True)
    z = ((df - mu) * lax.rsqrt(var + LN_EPS) * ln_g.astype(jnp.float32)
         + ln_b.astype(jnp.float32)).astype(h.dtype)
    return jax.nn.silu(z) @ w_pw2 + b_pw2, u_ext[:, -CONV_STATE:]


def _swiglu(h, wg, wu, wd):
    return (jax.nn.silu(h @ wg) * (h @ wu)) @ wd


def _trunk(x, past, past_len, params):
    (rel_bias, norm_mix, norm_ffn, norm_final, attn_wq, attn_wk, attn_wv, attn_wo,
     idx_wq, idx_wk, idx_ww, conv_w_pw1, conv_b_pw1, conv_w_dw, conv_b_dw, conv_ln_g,
     conv_ln_b, conv_w_pw2, conv_b_pw2, ffn_w_gate, ffn_w_up, ffn_w_down) = params
    h = x
    new_k, new_v, new_ki, new_conv = [], [], [], []
    ia = 0
    ic = 0
    for i in range(DEPTH):
        hn = _rmsnorm(h, norm_mix[i])
        if i % N_MIXERS == 0:
            p_att = None if past is None else (past[0][ia], past[1][ia], past[2][ia])
            mix, k, v, ki = _dsa_mixer(hn, p_att, past_len, rel_bias, attn_wq[ia], attn_wk[ia],
                                       attn_wv[ia], attn_wo[ia], idx_wq[ia], idx_wk[ia], idx_ww[ia])
            new_k.append(k)
            new_v.append(v)
            new_ki.append(ki)
            ia += 1
        else:
            p_conv = None if past is None else past[3][ic]
            mix, cs = _conv_mixer(hn, p_conv, conv_w_pw1[ic], conv_b_pw1[ic], conv_w_dw[ic],
                                  conv_b_dw[ic], conv_ln_g[ic], conv_ln_b[ic], conv_w_pw2[ic],
                                  conv_b_pw2[ic])
            new_conv.append(cs)
            ic += 1
        h = h + mix
        h = h + _swiglu(_rmsnorm(h, norm_ffn[i]), ffn_w_gate[i], ffn_w_up[i], ffn_w_down[i])
    y = _rmsnorm(h, norm_final)
    return y, jnp.stack(new_k), jnp.stack(new_v), jnp.stack(new_ki), jnp.stack(new_conv)


def setup_inputs(seed: int = 0) -> dict:
    key = jax.random.key(seed)
    ks = jax.random.split(key, 32)

    def nrm(k, shape, scale):
        return jax.random.normal(k, shape, jnp.float32) * scale

    d = D_MODEL
    hd = N_HEADS * HEAD_DIM
    kvd = N_KV_HEADS * HEAD_DIM
    return {
        'x_prompt': nrm(ks[0], (BATCH, SEQ, d), 1.0),
        'x_sample': nrm(ks[1], (DEC_BATCH, DEC_SEQ, d), 1.0),
        'cache_k': nrm(ks[2], (N_ATTN_LAYERS, DEC_BATCH, PAST_LEN, N_KV_HEADS, HEAD_DIM), 1.0),
        'cache_v': nrm(ks[3], (N_ATTN_LAYERS, DEC_BATCH, PAST_LEN, N_KV_HEADS, HEAD_DIM), 1.0),
        'cache_kidx': nrm(ks[4], (N_ATTN_LAYERS, DEC_BATCH, PAST_LEN, IDX_DIM), 1.0),
        'state_conv': nrm(ks[5], (N_CONV_LAYERS, DEC_BATCH, CONV_STATE, CONV_DIM), 0.5),
        'rel_bias': nrm(ks[6], (N_BUCKETS, N_HEADS), 0.5),
        'norm_mix': 1.0 + nrm(ks[7], (DEPTH, d), 0.01),
        'norm_ffn': 1.0 + nrm(ks[8], (DEPTH, d), 0.01),
        'norm_final': 1.0 + nrm(ks[9], (d,), 0.01),
        'attn_wq': nrm(ks[10], (N_ATTN_LAYERS, d, hd), d ** -0.5),
        'attn_wk': nrm(ks[11], (N_ATTN_LAYERS, d, kvd), d ** -0.5),
        'attn_wv': nrm(ks[12], (N_ATTN_LAYERS, d, kvd), d ** -0.5),
        'attn_wo': nrm(ks[13], (N_ATTN_LAYERS, hd, d), hd ** -0.5),
        'idx_wq': nrm(ks[14], (N_ATTN_LAYERS, d, N_IDX_HEADS * IDX_DIM), d ** -0.5),
        'idx_wk': nrm(ks[15], (N_ATTN_LAYERS, d, IDX_DIM), d ** -0.5),
        'idx_ww': nrm(ks[16], (N_ATTN_LAYERS, d, N_IDX_HEADS), d ** -0.5),
        'conv_w_pw1': nrm(ks[17], (N_CONV_LAYERS, d, 2 * CONV_DIM), d ** -0.5),
        'conv_b_pw1': nrm(ks[18], (N_CONV_LAYERS, 2 * CONV_DIM), 0.01),
        'conv_w_dw': nrm(ks[19], (N_CONV_LAYERS, DW_WIDTH, CONV_DIM), DW_WIDTH ** -0.5),
        'conv_b_dw': nrm(ks[20], (N_CONV_LAYERS, CONV_DIM), 0.01),
        'conv_ln_g': 1.0 + nrm(ks[21], (N_CONV_LAYERS, CONV_DIM), 0.01),
        'conv_ln_b': nrm(ks[22], (N_CONV_LAYERS, CONV_DIM), 0.01),
        'conv_w_pw2': nrm(ks[23], (N_CONV_LAYERS, CONV_DIM, d), CONV_DIM ** -0.5),
        'conv_b_pw2': nrm(ks[24], (N_CONV_LAYERS, d), 0.01),
        'ffn_w_gate': nrm(ks[25], (DEPTH, d, D_FF), d ** -0.5),
        'ffn_w_up': nrm(ks[26], (DEPTH, d, D_FF), d ** -0.5),
        'ffn_w_down': nrm(ks[27], (DEPTH, D_FF, d), D_FF ** -0.5),
    }


def reference(x_prompt, x_sample, cache_k, cache_v, cache_kidx, state_conv, rel_bias,
              norm_mix, norm_ffn, norm_final, attn_wq, attn_wk, attn_wv, attn_wo,
              idx_wq, idx_wk, idx_ww, conv_w_pw1, conv_b_pw1, conv_w_dw, conv_b_dw,
              conv_ln_g, conv_ln_b, conv_w_pw2, conv_b_pw2, ffn_w_gate, ffn_w_up, ffn_w_down):
    params = (rel_bias, norm_mix, norm_ffn, norm_final, attn_wq, attn_wk, attn_wv, attn_wo,
              idx_wq, idx_wk, idx_ww, conv_w_pw1, conv_b_pw1, conv_w_dw, conv_b_dw, conv_ln_g,
              conv_ln_b, conv_w_pw2, conv_b_pw2, ffn_w_gate, ffn_w_up, ffn_w_down)
    y_prompt, k_prompt, v_prompt, kidx_prompt, conv_prompt = _trunk(x_prompt, None, 0, params)
    y_sample, k_sample, v_sample, kidx_sample, conv_sample = _trunk(
        x_sample, (cache_k, cache_v, cache_kidx, state_conv), cache_k.shape[2], params)
    return (y_prompt, y_sample, k_prompt, v_prompt, kidx_prompt, conv_prompt,
            k_sample, v_sample, kidx_sample, conv_sample)
```

```python
import functools
import math

import numpy as np
import jax
import jax.numpy as jnp
from jax import lax
from jax.experimental import pallas as pl
from jax.experimental.pallas import tpu as pltpu

CHUNK = 64
N_HEADS = 16
N_KV_HEADS = 4
GROUP = N_HEADS // N_KV_HEADS
HEAD_DIM = 128
N_IDX_HEADS = 16
IDX_DIM = 64
IDX_SCALE = (N_IDX_HEADS ** -0.5) * (IDX_DIM ** -0.5)
TOPK_MAX = 256
N_BUCKETS = 32
DW_WIDTH = 31
CONV_STATE = DW_WIDTH - 1
RMS_EPS = 1e-6
LN_EPS = 1e-5

LANES = 128
HALO = 32
MASK_NEG = -1e30
INT_MIN = -(2 ** 31)
VMEM_LIMIT = 48 * 1024 * 1024

F32 = jnp.float32
BF16 = jnp.bfloat16


def _cparams(*sem):
    return pltpu.CompilerParams(dimension_semantics=sem, vmem_limit_bytes=VMEM_LIMIT)


def _pick(n, prefs):
    for p in prefs:
        if n % p == 0:
            return p
    return n


def _rms(x, g, eps):
    return x * lax.rsqrt(jnp.mean(x * x, axis=-1, keepdims=True) + eps) * g


def _sigmoid(x):
    return 1.0 / (1.0 + jnp.exp(-x))


def _norm_mm_kernel(x_ref, g_ref, w_ref, o_ref, hn_ref, *, scale):
    @pl.when(pl.program_id(1) == 0)
    def _():
        hn_ref[...] = _rms(x_ref[...], g_ref[...], RMS_EPS).astype(BF16)

    y = jnp.dot(hn_ref[...], w_ref[...], preferred_element_type=F32)
    if scale != 1.0:
        y = y * scale
    o_ref[...] = y.astype(o_ref.dtype)


def _norm_matmul(x, gain, w, out_dtype, scale=1.0):
    n, d = x.shape
    nout = w.shape[1]
    tm = _pick(n, (512, 256, 128, 64, 32, 16, 8))
    tn = _pick(nout, (512, 384, 256, 128))
    return pl.pallas_call(
        functools.partial(_norm_mm_kernel, scale=scale),
        grid=(n // tm, nout // tn),
        in_specs=[pl.BlockSpec((tm, d), lambda i, j: (i, 0)),
                  pl.BlockSpec((1, d), lambda i, j: (0, 0)),
                  pl.BlockSpec((d, tn), lambda i, j: (0, j))],
        out_specs=pl.BlockSpec((tm, tn), lambda i, j: (i, j)),
        out_shape=jax.ShapeDtypeStruct((n, nout), out_dtype),
        scratch_shapes=[pltpu.VMEM((tm, d), BF16)],
        compiler_params=_cparams("parallel", "arbitrary"),
        name="norm_matmul",
    )(x, gain.reshape(1, d), w)


def _norm_glu_kernel(x_ref, g_ref, wa_ref, wg_ref, ba_ref, bg_ref, o_ref, hn_ref):
    @pl.when(pl.program_id(1) == 0)
    def _():
        hn_ref[...] = _rms(x_ref[...], g_ref[...], RMS_EPS).astype(BF16)

    hn = hn_ref[...]
    a = jnp.dot(hn, wa_ref[...], preferred_element_type=F32) + ba_ref[...]
    gt = jnp.dot(hn, wg_ref[...], preferred_element_type=F32) + bg_ref[...]
    o_ref[...] = a * _sigmoid(gt)


def _norm_glu(x, gain, w, b):
    n, d = x.shape
    c = w.shape[1] // 2
    tm = _pick(n, (512, 256, 128, 64, 32, 16, 8))
    tn = _pick(c, (512, 256, 128))
    nj = c // tn
    b2 = b.reshape(1, 2 * c)
    return pl.pallas_call(
        _norm_glu_kernel,
        grid=(n // tm, nj),
        in_specs=[pl.BlockSpec((tm, d), lambda i, j: (i, 0)),
                  pl.BlockSpec((1, d), lambda i, j: (0, 0)),
                  pl.BlockSpec((d, tn), lambda i, j: (0, j)),
                  pl.BlockSpec((d, tn), lambda i, j: (0, j + nj)),
                  pl.BlockSpec((1, tn), lambda i, j: (0, j)),
                  pl.BlockSpec((1, tn), lambda i, j: (0, j + nj))],
        out_specs=pl.BlockSpec((tm, tn), lambda i, j: (i, j)),
        out_shape=jax.ShapeDtypeStruct((n, c), F32),
        scratch_shapes=[pltpu.VMEM((tm, d), BF16)],
        compiler_params=_cparams("parallel", "arbitrary"),
        name="norm_glu",
    )(x, gain.reshape(1, d), w, w, b2, b2)


def _mm_res_kernel(a_ref, w_ref, b_ref, r_ref, o_ref):
    o_ref[...] = r_ref[...] + jnp.dot(a_ref[...], w_ref[...], preferred_element_type=F32) + b_ref[...]


def _matmul_residual(a, w, bias, res):
    n, k = a.shape
    nout = w.shape[1]
    tm = _pick(n, (512, 256, 128, 64, 32, 16))
    tn = _pick(nout, (512, 256, 128))
    return pl.pallas_call(
        _mm_res_kernel,
        grid=(n // tm, nout // tn),
        in_specs=[pl.BlockSpec((tm, k), lambda i, j: (i, 0)),
                  pl.BlockSpec((k, tn), lambda i, j: (0, j)),
                  pl.BlockSpec((1, tn), lambda i, j: (0, j)),
                  pl.BlockSpec((tm, tn), lambda i, j: (i, j))],
        out_specs=pl.BlockSpec((tm, tn), lambda i, j: (i, j)),
        out_shape=jax.ShapeDtypeStruct((n, nout), F32),
        compiler_params=_cparams("parallel", "arbitrary"),
        name="matmul_residual",
    )(a, w, bias.reshape(1, nout), res)


def _ffn_kernel(x_ref, g_ref, wg_ref, wu_ref, wd_ref, gf_ref, o_ref, hn_ref, acc_ref, *, final_norm):
    j = pl.program_id(1)

    @pl.when(j == 0)
    def _():
        hn_ref[...] = _rms(x_ref[...], g_ref[...], RMS_EPS).astype(BF16)
        acc_ref[...] = jnp.zeros_like(acc_ref)

    hn = hn_ref[...]
    a = jnp.dot(hn, wg_ref[...], preferred_element_type=F32)
    u = jnp.dot(hn, wu_ref[...], preferred_element_type=F32)
    act = (a * _sigmoid(a)) * u
    acc_ref[...] += jnp.dot(act.astype(BF16), wd_ref[...], preferred_element_type=F32)

    @pl.when(j == pl.num_programs(1) - 1)
    def _():
        y = x_ref[...] + acc_ref[...]
        if final_norm:
            y = _rms(y, gf_ref[...], RMS_EPS)
        o_ref[...] = y


def _ffn(x, gain, wg, wu, wd, gain_final, final_norm):
    n, d = x.shape
    f = wg.shape[1]
    tm = _pick(n, (512, 256, 128, 64, 32, 16, 8))
    tf = _pick(f, (512, 256, 128))
    return pl.pallas_call(
        functools.partial(_ffn_kernel, final_norm=final_norm),
        grid=(n // tm, f // tf),
        in_specs=[pl.BlockSpec((tm, d), lambda i, j: (i, 0)),
                  pl.BlockSpec((1, d), lambda i, j: (0, 0)),
                  pl.BlockSpec((d, tf), lambda i, j: (0, j)),
                  pl.BlockSpec((d, tf), lambda i, j: (0, j)),
                  pl.BlockSpec((tf, d), lambda i, j: (j, 0)),
                  pl.BlockSpec((1, d), lambda i, j: (0, 0))],
        out_specs=pl.BlockSpec((tm, d), lambda i, j: (i, 0)),
        out_shape=jax.ShapeDtypeStruct((n, d), F32),
        scratch_shapes=[pltpu.VMEM((tm, d), BF16), pltpu.VMEM((tm, d), F32)],
        compiler_params=_cparams("parallel", "arbitrary"),
        name="ffn",
    )(x, gain.reshape(1, d), wg, wu, wd, gain_final.reshape(1, d))


def _dwconv_kernel(u_ref, halo_ref, past_ref, w_ref, b_ref, lg_ref, lb_ref, o_ref, win_ref, dw_ref, *, tt, c):
    halo = jnp.where(pl.program_id(1) == 0, past_ref[0], halo_ref[0])
    win_ref[0:HALO, :] = halo
    win_ref[HALO:HALO + tt, :] = u_ref[0]

    def chunk(ci, carry):
        off = pl.multiple_of(ci * LANES, LANES)
        acc = jnp.zeros((tt, LANES), F32) + b_ref[:, pl.ds(off, LANES)]
        for k in range(DW_WIDTH):
            acc = acc + w_ref[k:k + 1, pl.ds(off, LANES)] * win_ref[pl.ds(k + HALO - CONV_STATE, tt), pl.ds(off, LANES)]
        dw_ref[:, pl.ds(off, LANES)] = acc
        return carry

    lax.fori_loop(0, c // LANES, chunk, 0)
    df = dw_ref[...]
    mu = jnp.mean(df, axis=-1, keepdims=True)
    var = jnp.mean(jnp.square(df - mu), axis=-1, keepdims=True)
    z = (df - mu) * lax.rsqrt(var + LN_EPS) * lg_ref[...] + lb_ref[...]
    o_ref[0] = (z * _sigmoid(z)).astype(BF16)


def _dwconv_ln_swish(u, past32, w_dw, b_dw, ln_g, ln_b):
    bsz, t, c = u.shape
    tt = _pick(t, (256, 128, 64, 32))
    w32 = jnp.concatenate([w_dw, jnp.zeros((HALO - DW_WIDTH, c), F32)], axis=0)
    r = tt // HALO
    return pl.pallas_call(
        functools.partial(_dwconv_kernel, tt=tt, c=c),
        grid=(bsz, t // tt),
        in_specs=[pl.BlockSpec((1, tt, c), lambda b, i: (b, i, 0)),
                  pl.BlockSpec((1, HALO, c), lambda b, i: (b, jnp.maximum(i * r - 1, 0), 0)),
                  pl.BlockSpec((1, HALO, c), lambda b, i: (b, 0, 0)),
                  pl.BlockSpec((HALO, c), lambda b, i: (0, 0)),
                  pl.BlockSpec((1, c), lambda b, i: (0, 0)),
                  pl.BlockSpec((1, c), lambda b, i: (0, 0)),
                  pl.BlockSpec((1, c), lambda b, i: (0, 0))],
        out_specs=pl.BlockSpec((1, tt, c), lambda b, i: (b, i, 0)),
        out_shape=jax.ShapeDtypeStruct((bsz, t, c), BF16),
        scratch_shapes=[pltpu.VMEM((tt + HALO, c), F32), pltpu.VMEM((tt, c), F32)],
        compiler_params=_cparams("parallel", "arbitrary"),
        name="dwconv_ln_swish",
    )(u, u, past32, w32, b_dw.reshape(1, c), ln_g.reshape(1, c), ln_b.reshape(1, c))


def _adm_end(qpos, l_valid):
    return jnp.minimum((qpos // CHUNK + 1) * CHUNK, l_valid)


def _index_mask_kernel(qi_ref, wi_ref, ka_ref, kb_ref, o_ref, s_ref, wb_ref, *,
                       tq, tk, lp, q_pos0, l_valid, n_keep):
    i = pl.program_id(1)
    q0 = q_pos0 + i * tq
    n_kt = (_adm_end(q0 + tq - 1, l_valid) + tk - 1) // tk
    n_all = lp // tk
    rows = q0 + lax.broadcasted_iota(jnp.int32, (tq, 1), 0)
    row_end = _adm_end(rows, l_valid)
    lane = lax.broadcasted_iota(jnp.int32, (tq, tk), 1)
    nrep = tk // LANES

    for h in range(N_IDX_HEADS):
        wb_ref[h] = jnp.broadcast_to(wi_ref[0][:, h:h + 1], (tq, LANES))

    def score_tile(kt, carry):
        off = pl.multiple_of(kt * tk, tk)
        ka = ka_ref[0, pl.ds(off, tk), :]
        kb = kb_ref[0, pl.ds(off, tk), :]
        acc = jnp.zeros((tq, tk), F32)
        for p in range(N_IDX_HEADS // 2):
            q2 = qi_ref[0, :, 2 * IDX_DIM * p:2 * IDX_DIM * (p + 1)]
            for half, kk in ((0, ka), (1, kb)):
                d = lax.dot_general(q2, kk, (((1,), (1,)), ((), ())), preferred_element_type=F32)
                w = jnp.concatenate([wb_ref[2 * p + half]] * nrep, axis=1)
                acc = acc + w * jnp.maximum(d, 0.0)
        bits = lax.bitcast_convert_type(acc, jnp.int32)
        key = bits ^ ((bits >> 31) & jnp.int32(0x7FFFFFFF))
        key = jnp.where(lane + off < row_end, key, jnp.int32(INT_MIN))
        s_ref[:, pl.ds(off, tk)] = key
        return carry

    lax.fori_loop(0, n_kt, score_tile, 0)

    def count(pred):
        def body(kt, acc):
            off = pl.multiple_of(kt * tk, tk)
            part = jnp.where(pred(s_ref[:, pl.ds(off, tk)], off), 1.0, 0.0)
            for r in range(nrep):
                acc = acc + part[:, r * LANES:(r + 1) * LANES]
            return acc
        acc = lax.fori_loop(0, n_kt, body, jnp.zeros((tq, LANES), F32))
        return jnp.sum(acc, axis=1, keepdims=True)

    def bit_step(s, u):
        cand = u | lax.shift_left(jnp.int32(1), 31 - s)
        ct = cand ^ jnp.int32(INT_MIN)
        c = count(lambda blk, off: blk >= ct)
        return jnp.where(c >= n_keep, cand, u)

    u = lax.fori_loop(0, 32, bit_step, jnp.zeros((tq, 1), jnp.int32))
    thr = u ^ jnp.int32(INT_MIN)
    c_gt = count(lambda blk, off: blk > thr)
    c_eq = count(lambda blk, off: blk == thr)
    need = n_keep - c_gt
    unfilled = thr == jnp.int32(INT_MIN)
    tie_rows = jnp.logical_and(c_eq > need, jnp.logical_not(unfilled))
    any_tie = jnp.max(jnp.where(tie_rows, 1.0, 0.0)) > 0.0

    nbits = max(1, int(lp).bit_length())

    def tie_search():
        def pos_step(s, p):
            cand = p | lax.shift_left(jnp.int32(1), nbits - 1 - s)
            c = count(lambda blk, off: jnp.logical_and(blk == thr, lane + off < cand))
            return jnp.where(c <= need, cand, p)
        return lax.fori_loop(0, nbits, pos_step, jnp.zeros((tq, 1), jnp.int32))

    p_cut = lax.cond(any_tie, tie_search, lambda: jnp.full((tq, 1), 2 ** nbits - 1, jnp.int32))
    p_cut = jnp.where(unfilled, 0, p_cut)

    def write_tile(kt, carry):
        off = pl.multiple_of(kt * tk, tk)
        blk = s_ref[:, pl.ds(off, tk)]
        keep = jnp.logical_or(blk > thr, jnp.logical_and(blk == thr, lane + off < p_cut))
        o_ref[0, :, pl.ds(off, tk)] = jnp.where(keep, 0.0, MASK_NEG).astype(BF16)
        return carry

    def blank_tile(kt, carry):
        off = pl.multiple_of(kt * tk, tk)
        o_ref[0, :, pl.ds(off, tk)] = jnp.full((tq, tk), MASK_NEG, BF16)
        return carry

    lax.fori_loop(0, n_kt, write_tile, 0)
    lax.fori_loop(n_kt, n_all, blank_tile, 0)


def _index_mask(qi, wi, ka, kb, *, q_pos0, l_valid, n_keep, tk):
    bsz, t, _ = qi.shape
    lp = ka.shape[1]
    tq = _pick(t, (128, 64, 32, 16))
    kern = functools.partial(_index_mask_kernel, tq=tq, tk=tk, lp=lp, q_pos0=q_pos0,
                             l_valid=l_valid, n_keep=n_keep)
    return pl.pallas_call(
        kern,
        grid=(bsz, t // tq),
        in_specs=[pl.BlockSpec((1, tq, N_IDX_HEADS * IDX_DIM), lambda b, i: (b, i, 0)),
                  pl.BlockSpec((1, tq, N_IDX_HEADS), lambda b, i: (b, i, 0)),
                  pl.BlockSpec((1, lp, 2 * IDX_DIM), lambda b, i: (b, 0, 0)),
                  pl.BlockSpec((1, lp, 2 * IDX_DIM), lambda b, i: (b, 0, 0))],
        out_specs=pl.BlockSpec((1, tq, lp), lambda b, i: (b, i, 0)),
        out_shape=jax.ShapeDtypeStruct((bsz, t, lp), BF16),
        scratch_shapes=[pltpu.VMEM((tq, lp), jnp.int32), pltpu.VMEM((N_IDX_HEADS, tq, LANES), F32)],
        compiler_params=_cparams("parallel", "arbitrary"),
        name="index_mask",
    )(qi, wi, ka, kb)


_BUCKET_STARTS = tuple(int(math.ceil(8 * 16 ** ((b - 8) / 8) - 1e-9)) for b in range(9, 16))
FAR_BUCKET = N_BUCKETS // 2 - 1


def _bias_tile_kernel(delta_ref, tbl_ref, o_ref, *, tq, tk):
    case = pl.program_id(0)
    h = pl.program_id(1)
    rel = (delta_ref[case] + lax.broadcasted_iota(jnp.int32, (tq, tk), 1)
           - lax.broadcasted_iota(jnp.int32, (tq, tk), 0))
    n = jnp.abs(rel)
    large = jnp.full((tq, tk), 8, jnp.int32)
    for st in _BUCKET_STARTS:
        large = large + jnp.where(n >= st, 1, 0)
    bucket = jnp.where(rel > 0, N_BUCKETS // 2, 0) + jnp.where(n < 8, n, large)

    def body(b, acc):
        return jnp.where(bucket == b, tbl_ref[b * N_HEADS + h], acc)

    acc = lax.fori_loop(0, N_BUCKETS, body, jnp.zeros((tq, tk), F32))
    o_ref[0, 0] = acc - tbl_ref[FAR_BUCKET * N_HEADS + h]


def _bias_tiles(rel_bias, deltas, tq, tk):
    ncase = len(deltas)
    return pl.pallas_call(
        functools.partial(_bias_tile_kernel, tq=tq, tk=tk),
        grid_spec=pltpu.PrefetchScalarGridSpec(
            num_scalar_prefetch=2,
            grid=(ncase, N_HEADS),
            in_specs=[],
            out_specs=pl.BlockSpec((1, 1, tq, tk), lambda c, h, d, t: (c, h, 0, 0)),
        ),
        out_shape=jax.ShapeDtypeStruct((ncase, N_HEADS, tq, tk), F32),
        compiler_params=_cparams("arbitrary", "arbitrary"),
        name="bias_tiles",
    )(jnp.asarray(deltas, jnp.int32), rel_bias.reshape(-1))


def _attn_kernel(qb_ref, kt_ref, case_ref, first_ref, last_ref,
                 q_ref, k_ref, v_ref, mask_ref, bias_ref, o_ref, acc_ref, m_ref, l_ref, *, tq, tk):
    s_id = pl.program_id(1)

    @pl.when(first_ref[s_id] == 1)
    def _():
        acc_ref[...] = jnp.zeros_like(acc_ref)
        m_ref[...] = jnp.full_like(m_ref, -jnp.inf)
        l_ref[...] = jnp.zeros_like(l_ref)

    maskb = mask_ref[0].astype(F32)
    has_bias = case_ref[s_id] >= 0

    for g in range(N_KV_HEADS):
        h0 = g * GROUP
        qg = jnp.concatenate(
            [q_ref[0, :, (h0 + j) * HEAD_DIM:(h0 + j + 1) * HEAD_DIM] for j in range(GROUP)], axis=0)
        kg = k_ref[0, :, g * HEAD_DIM:(g + 1) * HEAD_DIM]
        vg = v_ref[0, :, g * HEAD_DIM:(g + 1) * HEAD_DIM]
        s = lax.dot_general(qg, kg, (((1,), (1,)), ((), ())), preferred_element_type=F32)
        s = s.reshape(GROUP, tq, tk) + maskb[None]
        s = lax.cond(has_bias, lambda x: x + bias_ref[0, h0:h0 + GROUP], lambda x: x, s)
        m_prev = m_ref[h0:h0 + GROUP]
        m_new = jnp.maximum(m_prev, jnp.max(s, axis=-1, keepdims=True))
        alpha = jnp.exp(m_prev - m_new)
        p = jnp.exp(s - m_new)
        l_ref[h0:h0 + GROUP] = alpha * l_ref[h0:h0 + GROUP] + jnp.sum(p, axis=-1, keepdims=True)
        m_ref[h0:h0 + GROUP] = m_new
        pv = jnp.dot(p.astype(BF16).reshape(GROUP * tq, tk), vg, preferred_element_type=F32)
        acc_ref[h0:h0 + GROUP] = alpha * acc_ref[h0:h0 + GROUP] + pv.reshape(GROUP, tq, HEAD_DIM)

    @pl.when(last_ref[s_id] == 1)
    def _():
        for h in range(N_HEADS):
            o_ref[0, :, h * HEAD_DIM:(h + 1) * HEAD_DIM] = (acc_ref[h] / l_ref[h]).astype(BF16)


def _attention(q, k, v, mask, rel_bias, *, q_pos0, l_valid):
    bsz, t, _ = q.shape
    lp = k.shape[1]
    tq = _pick(t, (256, 128, 64, 32, 16))
    tk = _pick(lp, (512, 256, 128))
    qb, kt, delta, first, last = [], [], [], [], []
    for i in range(t // tq):
        qlast = q_pos0 + i * tq + tq - 1
        n_kt = -(-min((qlast // CHUNK + 1) * CHUNK, l_valid) // tk)
        for j in range(n_kt):
            qb.append(i)
            kt.append(j)
            delta.append(j * tk - (q_pos0 + i * tq))
            first.append(int(j == 0))
            last.append(int(j == n_kt - 1))
    need = [d + tk - 1 > -LANES for d in delta]
    deltas = sorted({d for d, nd in zip(delta, need) if nd})
    case = [deltas.index(d) if nd else -1 for d, nd in zip(delta, need)]
    fetch, cur = [], 0
    for cs in case:
        cur = cs if cs >= 0 else cur
        fetch.append(cur)
    bias = _bias_tiles(rel_bias, deltas, tq, tk)
    tabs = [jnp.asarray(a, jnp.int32) for a in (qb, kt, case, first, last, fetch)]
    d = N_HEADS * HEAD_DIM
    dkv = N_KV_HEADS * HEAD_DIM

    def kern(qb_ref, kt_ref, case_ref, first_ref, last_ref, fetch_ref, *refs):
        _attn_kernel(qb_ref, kt_ref, case_ref, first_ref, last_ref, *refs, tq=tq, tk=tk)

    return pl.pallas_call(
        kern,
        grid_spec=pltpu.PrefetchScalarGridSpec(
            num_scalar_prefetch=6,
            grid=(bsz, len(qb)),
            in_specs=[pl.BlockSpec((1, tq, d), lambda b, s, qb, kt, cs, fi, la, fe: (b, qb[s], 0)),
                      pl.BlockSpec((1, tk, dkv), lambda b, s, qb, kt, cs, fi, la, fe: (b, kt[s], 0)),
                      pl.BlockSpec((1, tk, dkv), lambda b, s, qb, kt, cs, fi, la, fe: (b, kt[s], 0)),
                      pl.BlockSpec((1, tq, tk), lambda b, s, qb, kt, cs, fi, la, fe: (b, qb[s], kt[s])),
                      pl.BlockSpec((1, N_HEADS, tq, tk), lambda b, s, qb, kt, cs, fi, la, fe: (fe[s], 0, 0, 0))],
            out_specs=pl.BlockSpec((1, tq, d), lambda b, s, qb, kt, cs, fi, la, fe: (b, qb[s], 0)),
            scratch_shapes=[pltpu.VMEM((N_HEADS, tq, HEAD_DIM), F32),
                            pltpu.VMEM((N_HEADS, tq, 1), F32),
                            pltpu.VMEM((N_HEADS, tq, 1), F32)],
        ),
        out_shape=jax.ShapeDtypeStruct((bsz, t, d), BF16),
        compiler_params=_cparams("parallel", "arbitrary"),
        name="attention",
    )(*tabs, q, k, v, mask, bias)


def _pad_rows(a, lp):
    return jnp.pad(a, ((0, 0), (0, lp - a.shape[1]), (0, 0)))


def _dsa_layer(h, past, past_len, rel_bias, gain, wq, wkv, widx, wo):
    bsz, t, d = h.shape
    hf = h.reshape(bsz * t, d)
    dkv = N_KV_HEADS * HEAD_DIM
    q = _norm_matmul(hf, gain, wq, BF16, scale=HEAD_DIM ** -0.5).reshape(bsz, t, d)
    kv = _norm_matmul(hf, gain, wkv, F32).reshape(bsz, t, 2 * dkv)
    idx = _norm_matmul(hf, gain, widx, F32).reshape(bsz, t, -1)
    k, v = kv[..., :dkv], kv[..., dkv:]
    nqi = N_IDX_HEADS * IDX_DIM
    qi = idx[..., :nqi].astype(BF16)
    ki = idx[..., nqi:nqi + IDX_DIM]
    wi = idx[..., nqi + IDX_DIM:nqi + IDX_DIM + N_IDX_HEADS] * IDX_SCALE
    if past is None:
        k_all, v_all, ki_all = k, v, ki
    else:
        k_all = jnp.concatenate([past[0].reshape(bsz, past_len, dkv), k], axis=1)
        v_all = jnp.concatenate([past[1].reshape(bsz, past_len, dkv), v], axis=1)
        ki_all = jnp.concatenate([past[2], ki], axis=1)
    l_valid = past_len + t
    n_keep = min(TOPK_MAX, l_valid // 4)
    tk = 512 if l_valid % 512 == 0 else 256
    lp = -(-l_valid // tk) * tk
    kb16 = _pad_rows(k_all.astype(BF16), lp)
    vb16 = _pad_rows(v_all.astype(BF16), lp)
    ki16 = _pad_rows(ki_all.astype(BF16), lp)
    zeros = jnp.zeros_like(ki16)
    ka = jnp.concatenate([ki16, zeros], axis=-1)
    kb = jnp.concatenate([zeros, ki16], axis=-1)
    mask = _index_mask(qi, wi, ka, kb, q_pos0=past_len, l_valid=l_valid, n_keep=n_keep, tk=256)
    o = _attention(q, kb16, vb16, mask, rel_bias, q_pos0=past_len, l_valid=l_valid)
    hout = _matmul_residual(o.reshape(bsz * t, d), wo, jnp.zeros((d,), F32), hf).reshape(bsz, t, d)
    return hout, k, v, ki


def _conv_layer(h, past, gain, w_pw1, b_pw1, w_dw, b_dw, ln_g, ln_b, w_pw2, b_pw2):
    bsz, t, d = h.shape
    hf = h.reshape(bsz * t, d)
    u = _norm_glu(hf, gain, w_pw1, b_pw1).reshape(bsz, t, d)
    if past is None:
        past = jnp.zeros((bsz, CONV_STATE, d), F32)
    past32 = jnp.concatenate([jnp.zeros((bsz, HALO - CONV_STATE, d), F32), past], axis=1)
    z = _dwconv_ln_swish(u, past32, w_dw, b_dw, ln_g, ln_b)
    hout = _matmul_residual(z.reshape(bsz * t, d), w_pw2, b_pw2, hf).reshape(bsz, t, d)
    state = jnp.concatenate([past, u], axis=1)[:, -CONV_STATE:]
    return hout, state


def _trunk(x, past, past_len, p):
    bsz, t, d = x.shape
    h, k, v, ki = _dsa_layer(x, None if past is None else past[:3], past_len, p["rel_bias"],
                             p["norm_mix"][0], p["wq"], p["wkv"], p["widx"], p["wo"])
    h = _ffn(h.reshape(bsz * t, d), p["norm_ffn"][0], p["wg"][0], p["wu"][0], p["wd"][0],
             p["norm_final"], False).reshape(bsz, t, d)
    h, cs = _conv_layer(h, None if past is None else past[3], p["norm_mix"][1], p["w_pw1"], p["b_pw1"],
                        p["w_dw"], p["b_dw"], p["ln_g"], p["ln_b"], p["w_pw2"], p["b_pw2"])
    y = _ffn(h.reshape(bsz * t, d), p["norm_ffn"][1], p["wg"][1], p["wu"][1], p["wd"][1],
             p["norm_final"], True).reshape(bsz, t, d)
    return (y, k.reshape(1, bsz, t, N_KV_HEADS, HEAD_DIM), v.reshape(1, bsz, t, N_KV_HEADS, HEAD_DIM),
            ki[None], cs[None])


def kernel(x_prompt, x_sample, cache_k, cache_v, cache_kidx, state_conv, rel_bias, norm_mix, norm_ffn, norm_final, attn_wq, attn_wk, attn_wv, attn_wo, idx_wq, idx_wk, idx_ww, conv_w_pw1, conv_b_pw1, conv_w_dw, conv_b_dw, conv_ln_g, conv_ln_b, conv_w_pw2, conv_b_pw2, ffn_w_gate, ffn_w_up, ffn_w_down):
    d = x_prompt.shape[-1]
    nidx = N_IDX_HEADS * IDX_DIM + IDX_DIM + N_IDX_HEADS
    nidx_pad = -(-nidx // LANES) * LANES
    widx = jnp.concatenate([idx_wq[0], idx_wk[0], idx_ww[0], jnp.zeros((d, nidx_pad - nidx), F32)], axis=1)
    p = dict(
        rel_bias=rel_bias, norm_mix=norm_mix, norm_ffn=norm_ffn, norm_final=norm_final,
        wq=attn_wq[0].astype(BF16),
        wkv=jnp.concatenate([attn_wk[0], attn_wv[0]], axis=1).astype(BF16),
        widx=widx.astype(BF16),
        wo=attn_wo[0].astype(BF16),
        wg=ffn_w_gate.astype(BF16), wu=ffn_w_up.astype(BF16), wd=ffn_w_down.astype(BF16),
        w_pw1=conv_w_pw1[0].astype(BF16), b_pw1=conv_b_pw1[0], w_dw=conv_w_dw[0], b_dw=conv_b_dw[0],
        ln_g=conv_ln_g[0], ln_b=conv_ln_b[0], w_pw2=conv_w_pw2[0].astype(BF16), b_pw2=conv_b_pw2[0],
    )
    y_p, k_p, v_p, ki_p, cs_p = _trunk(x_prompt, None, 0, p)
    past = (cache_k[0], cache_v[0], cache_kidx[0], state_conv[0])
    y_s, k_s, v_s, ki_s, cs_s = _trunk(x_sample, past, cache_k.shape[2], p)
    return (y_p, y_s, k_p, v_p, ki_p, cs_p, k_s, v_s, ki_s, cs_s)
```

```python
import functools
import math

import numpy as np
import jax
import jax.numpy as jnp
from jax import lax
from jax.experimental import pallas as pl
from jax.experimental.pallas import tpu as pltpu

CHUNK = 64
N_HEADS = 16
N_KV_HEADS = 4
GROUP = N_HEADS // N_KV_HEADS
HEAD_DIM = 128
N_IDX_HEADS = 16
IDX_DIM = 64
IDX_SCALE = (N_IDX_HEADS ** -0.5) * (IDX_DIM ** -0.5)
TOPK_MAX = 256
N_BUCKETS = 32
DW_WIDTH = 31
CONV_STATE = DW_WIDTH - 1
RMS_EPS = 1e-6
LN_EPS = 1e-5

LANES = 128
HALO = 32
MASK_NEG = -1e30
LOG2E = math.log2(math.e)
INT_MIN = -(2 ** 31)
VMEM_LIMIT = 48 * 1024 * 1024

F32 = jnp.float32
BF16 = jnp.bfloat16


def _cparams(*sem):
    return pltpu.CompilerParams(dimension_semantics=sem, vmem_limit_bytes=VMEM_LIMIT)


def _pick(n, prefs):
    for p in prefs:
        if n % p == 0:
            return p
    return n


def _rms(x, g, eps):
    return x * lax.rsqrt(jnp.mean(x * x, axis=-1, keepdims=True) + eps) * g


def _sigmoid(x):
    return 1.0 / (1.0 + jnp.exp(-x))


def _norm_mm_kernel(x_ref, g_ref, w_ref, o_ref, hn_ref, *, scale):
    @pl.when(pl.program_id(1) == 0)
    def _():
        hn_ref[...] = _rms(x_ref[...], g_ref[...], RMS_EPS).astype(BF16)

    y = jnp.dot(hn_ref[...], w_ref[...], preferred_element_type=F32)
    if scale != 1.0:
        y = y * scale
    o_ref[...] = y.astype(o_ref.dtype)


def _norm_matmul(x, gain, w, out_dtype, scale=1.0):
    n, d = x.shape
    nout = w.shape[1]
    tm = _pick(n, (512, 256, 128, 64, 32, 16, 8))
    tn = _pick(nout, (512, 384, 256, 128))
    return pl.pallas_call(
        functools.partial(_norm_mm_kernel, scale=scale),
        grid=(n // tm, nout // tn),
        in_specs=[pl.BlockSpec((tm, d), lambda i, j: (i, 0)),
                  pl.BlockSpec((1, d), lambda i, j: (0, 0)),
                  pl.BlockSpec((d, tn), lambda i, j: (0, j))],
        out_specs=pl.BlockSpec((tm, tn), lambda i, j: (i, j)),
        out_shape=jax.ShapeDtypeStruct((n, nout), out_dtype),
        scratch_shapes=[pltpu.VMEM((tm, d), BF16)],
        compiler_params=_cparams("parallel", "arbitrary"),
        name="norm_matmul",
    )(x, gain.reshape(1, d), w)


def _norm_glu_kernel(x_ref, g_ref, wa_ref, wg_ref, ba_ref, bg_ref, o_ref, hn_ref):
    @pl.when(pl.program_id(1) == 0)
    def _():
        hn_ref[...] = _rms(x_ref[...], g_ref[...], RMS_EPS).astype(BF16)

    hn = hn_ref[...]
    a = jnp.dot(hn, wa_ref[...], preferred_element_type=F32) + ba_ref[...]
    gt = jnp.dot(hn, wg_ref[...], preferred_element_type=F32) + bg_ref[...]
    o_ref[...] = a * _sigmoid(gt)


def _norm_glu(x, gain, w, b):
    n, d = x.shape
    c = w.shape[1] // 2
    tm = _pick(n, (512, 256, 128, 64, 32, 16, 8))
    tn = _pick(c, (512, 256, 128))
    nj = c // tn
    b2 = b.reshape(1, 2 * c)
    return pl.pallas_call(
        _norm_glu_kernel,
        grid=(n // tm, nj),
        in_specs=[pl.BlockSpec((tm, d), lambda i, j: (i, 0)),
                  pl.BlockSpec((1, d), lambda i, j: (0, 0)),
                  pl.BlockSpec((d, tn), lambda i, j: (0, j)),
                  pl.BlockSpec((d, tn), lambda i, j: (0, j + nj)),
                  pl.BlockSpec((1, tn), lambda i, j: (0, j)),
                  pl.BlockSpec((1, tn), lambda i, j: (0, j + nj))],
        out_specs=pl.BlockSpec((tm, tn), lambda i, j: (i, j)),
        out_shape=jax.ShapeDtypeStruct((n, c), F32),
        scratch_shapes=[pltpu.VMEM((tm, d), BF16)],
        compiler_params=_cparams("parallel", "arbitrary"),
        name="norm_glu",
    )(x, gain.reshape(1, d), w, w, b2, b2)


def _mm_res_kernel(a_ref, w_ref, b_ref, r_ref, o_ref):
    o_ref[...] = r_ref[...] + jnp.dot(a_ref[...], w_ref[...], preferred_element_type=F32) + b_ref[...]


def _matmul_residual(a, w, bias, res):
    n, k = a.shape
    nout = w.shape[1]
    tm = _pick(n, (512, 256, 128, 64, 32, 16))
    tn = _pick(nout, (512, 256, 128))
    return pl.pallas_call(
        _mm_res_kernel,
        grid=(n // tm, nout // tn),
        in_specs=[pl.BlockSpec((tm, k), lambda i, j: (i, 0)),
                  pl.BlockSpec((k, tn), lambda i, j: (0, j)),
                  pl.BlockSpec((1, tn), lambda i, j: (0, j)),
                  pl.BlockSpec((tm, tn), lambda i, j: (i, j))],
        out_specs=pl.BlockSpec((tm, tn), lambda i, j: (i, j)),
        out_shape=jax.ShapeDtypeStruct((n, nout), F32),
        compiler_params=_cparams("parallel", "arbitrary"),
        name="matmul_residual",
    )(a, w, bias.reshape(1, nout), res)


def _ffn_kernel(x_ref, g_ref, wg_ref, wu_ref, wd_ref, gf_ref, o_ref, hn_ref, acc_ref, *, final_norm):
    j = pl.program_id(1)

    @pl.when(j == 0)
    def _():
        hn_ref[...] = _rms(x_ref[...], g_ref[...], RMS_EPS).astype(BF16)
        acc_ref[...] = jnp.zeros_like(acc_ref)

    hn = hn_ref[...]
    a = jnp.dot(hn, wg_ref[...], preferred_element_type=F32)
    u = jnp.dot(hn, wu_ref[...], preferred_element_type=F32)
    act = (a * _sigmoid(a)) * u
    acc_ref[...] += jnp.dot(act.astype(BF16), wd_ref[...], preferred_element_type=F32)

    @pl.when(j == pl.num_programs(1) - 1)
    def _():
        y = x_ref[...] + acc_ref[...]
        if final_norm:
            y = _rms(y, gf_ref[...], RMS_EPS)
        o_ref[...] = y


def _ffn(x, gain, wg, wu, wd, gain_final, final_norm):
    n, d = x.shape
    f = wg.shape[1]
    tm = _pick(n, (512, 256, 128, 64, 32, 16, 8))
    tf = _pick(f, (512, 256, 128))
    return pl.pallas_call(
        functools.partial(_ffn_kernel, final_norm=final_norm),
        grid=(n // tm, f // tf),
        in_specs=[pl.BlockSpec((tm, d), lambda i, j: (i, 0)),
                  pl.BlockSpec((1, d), lambda i, j: (0, 0)),
                  pl.BlockSpec((d, tf), lambda i, j: (0, j)),
                  pl.BlockSpec((d, tf), lambda i, j: (0, j)),
                  pl.BlockSpec((tf, d), lambda i, j: (j, 0)),
                  pl.BlockSpec((1, d), lambda i, j: (0, 0))],
        out_specs=pl.BlockSpec((tm, d), lambda i, j: (i, 0)),
        out_shape=jax.ShapeDtypeStruct((n, d), F32),
        scratch_shapes=[pltpu.VMEM((tm, d), BF16), pltpu.VMEM((tm, d), F32)],
        compiler_params=_cparams("parallel", "arbitrary"),
        name="ffn",
    )(x, gain.reshape(1, d), wg, wu, wd, gain_final.reshape(1, d))


def _dwconv_kernel(u_ref, halo_ref, past_ref, w_ref, b_ref, lg_ref, lb_ref, o_ref, win_ref, dw_ref, *, tt, c):
    halo = jnp.where(pl.program_id(1) == 0, past_ref[0], halo_ref[0])
    win_ref[0:HALO, :] = halo
    win_ref[HALO:HALO + tt, :] = u_ref[0]

    def chunk(ci, carry):
        off = pl.multiple_of(ci * LANES, LANES)
        acc = jnp.zeros((tt, LANES), F32) + b_ref[:, pl.ds(off, LANES)]
        for k in range(DW_WIDTH):
            acc = acc + w_ref[k:k + 1, pl.ds(off, LANES)] * win_ref[pl.ds(k + HALO - CONV_STATE, tt), pl.ds(off, LANES)]
        dw_ref[:, pl.ds(off, LANES)] = acc
        return carry

    lax.fori_loop(0, c // LANES, chunk, 0)
    df = dw_ref[...]
    mu = jnp.mean(df, axis=-1, keepdims=True)
    var = jnp.mean(jnp.square(df - mu), axis=-1, keepdims=True)
    z = (df - mu) * lax.rsqrt(var + LN_EPS) * lg_ref[...] + lb_ref[...]
    o_ref[0] = (z * _sigmoid(z)).astype(BF16)


def _dwconv_ln_swish(u, past32, w_dw, b_dw, ln_g, ln_b):
    bsz, t, c = u.shape
    tt = _pick(t, (256, 128, 64, 32))
    w32 = jnp.concatenate([w_dw, jnp.zeros((HALO - DW_WIDTH, c), F32)], axis=0)
    r = tt // HALO
    return pl.pallas_call(
        functools.partial(_dwconv_kernel, tt=tt, c=c),
        grid=(bsz, t // tt),
        in_specs=[pl.BlockSpec((1, tt, c), lambda b, i: (b, i, 0)),
                  pl.BlockSpec((1, HALO, c), lambda b, i: (b, jnp.maximum(i * r - 1, 0), 0)),
                  pl.BlockSpec((1, HALO, c), lambda b, i: (b, 0, 0)),
                  pl.BlockSpec((HALO, c), lambda b, i: (0, 0)),
                  pl.BlockSpec((1, c), lambda b, i: (0, 0)),
                  pl.BlockSpec((1, c), lambda b, i: (0, 0)),
                  pl.BlockSpec((1, c), lambda b, i: (0, 0))],
        out_specs=pl.BlockSpec((1, tt, c), lambda b, i: (b, i, 0)),
        out_shape=jax.ShapeDtypeStruct((bsz, t, c), BF16),
        scratch_shapes=[pltpu.VMEM((tt + HALO, c), F32), pltpu.VMEM((tt, c), F32)],
        compiler_params=_cparams("parallel", "arbitrary"),
        name="dwconv_ln_swish",
    )(u, u, past32, w32, b_dw.reshape(1, c), ln_g.reshape(1, c), ln_b.reshape(1, c))


def _adm_end(qpos, l_valid):
    return jnp.minimum((qpos // CHUNK + 1) * CHUNK, l_valid)


def _flip_magnitude(bits):
    return bits ^ ((bits >> 31) & jnp.int32(0x7FFFFFFF))


def _sort_key(x):
    return _flip_magnitude(lax.bitcast_convert_type(x, jnp.int32))


def _key_value(key):
    return lax.bitcast_convert_type(_flip_magnitude(key), F32)


def _index_mask_kernel(qi_ref, wi_ref, ka_ref, kb_ref, o_ref, s_ref, wb_ref, *,
                       tq, tk, cw, lp, q_pos0, l_valid, n_keep):
    i = pl.program_id(1)
    q0 = q_pos0 + i * tq
    n_kt = (_adm_end(q0 + tq - 1, l_valid) + tk - 1) // tk
    n_all = lp // tk
    rows = q0 + lax.broadcasted_iota(jnp.int32, (tq, 1), 0)
    row_end = _adm_end(rows, l_valid)
    lane = lax.broadcasted_iota(jnp.int32, (tq, tk), 1)
    nrep = tk // LANES

    for h in range(N_IDX_HEADS):
        wb_ref[h] = jnp.broadcast_to(wi_ref[0][:, h:h + 1], (tq, LANES))

    def score_tile(kt, carry):
        off = pl.multiple_of(kt * tk, tk)
        ka = ka_ref[0, pl.ds(off, tk), :]
        kb = kb_ref[0, pl.ds(off, tk), :]
        acc = jnp.zeros((tq, tk), F32)
        for p in range(N_IDX_HEADS // 2):
            q2 = qi_ref[0, :, 2 * IDX_DIM * p:2 * IDX_DIM * (p + 1)]
            for half, kk in ((0, ka), (1, kb)):
                d = lax.dot_general(q2, kk, (((1,), (1,)), ((), ())), preferred_element_type=F32)
                w = jnp.concatenate([wb_ref[2 * p + half]] * nrep, axis=1)
                acc = acc + w * jnp.maximum(d, 0.0)
        adm = lane + off < row_end
        s_ref[:, pl.ds(off, tk)] = jnp.where(adm, _sort_key(acc), jnp.int32(INT_MIN))
        return carry

    def pad_tile(kt, carry):
        s_ref[:, pl.ds(pl.multiple_of(kt * tk, tk), tk)] = jnp.full((tq, tk), INT_MIN, jnp.int32)
        return carry

    n_ct = (n_kt * tk + cw - 1) // cw
    lax.fori_loop(0, n_kt, score_tile, 0)
    lax.fori_loop(n_kt, n_ct * (cw // tk), pad_tile, 0)

    rows_h = _pick(tq, (64, 32, 16))

    def count(pred, args):
        parts = []
        for hb in range(tq // rows_h):
            rs = slice(hb * rows_h, (hb + 1) * rows_h)
            bargs = [jnp.broadcast_to(a[rs], (rows_h, LANES)) for a in args]
            lane_h = lax.broadcasted_iota(jnp.int32, (rows_h, LANES), 1)

            def body(ct, acc):
                for r in range(cw // LANES):
                    off = pl.multiple_of(ct * cw + r * LANES, LANES)
                    acc = acc + jnp.where(pred(s_ref[rs, pl.ds(off, LANES)], lane_h + off, *bargs), 1.0, 0.0)
                return acc
            acc = lax.fori_loop(0, n_ct, body, jnp.zeros((rows_h, LANES), F32))
            parts.append(jnp.sum(acc, axis=1, keepdims=True))
        return jnp.concatenate(parts, axis=0)

    kf = float(n_keep)
    probes_per_check = 4

    def probe(step, u, settled, thr_ge):
        cand = u | lax.shift_left(jnp.int32(1), 31 - step)
        ct = cand ^ jnp.int32(INT_MIN)
        c = count(lambda blk, pos, t: blk >= t, [ct])
        exact = jnp.logical_and(c == kf, settled == 0)
        return (jnp.where(c >= kf, cand, u), jnp.where(exact, 1, settled), jnp.where(exact, ct, thr_ge))

    def n_open(settled):
        return jnp.max(jnp.where(settled == 0, 1.0, 0.0))

    def probe_group(state):
        g, u, settled, thr_ge, _ = state
        for j in range(probes_per_check):
            u, settled, thr_ge = probe(g * probes_per_check + j, u, settled, thr_ge)
        return g + 1, u, settled, thr_ge, n_open(settled)

    settled0 = jnp.where(row_end <= n_keep, 1, 0)
    state0 = (jnp.int32(0), jnp.zeros((tq, 1), jnp.int32), settled0,
              jnp.full((tq, 1), INT_MIN + 1, jnp.int32), n_open(settled0))
    _, u, settled, thr_ge, _ = lax.while_loop(
        lambda st: jnp.logical_and(st[4] > 0.0, st[0] < 32 // probes_per_check), probe_group, state0)

    tie_rows = settled == 0
    thr_t = u ^ jnp.int32(INT_MIN)
    nbits = max(1, int(lp).bit_length())

    def tie_search():
        need = kf - count(lambda blk, pos, t: blk > t, [thr_t])

        def pos_step(s, p):
            cand = p | lax.shift_left(jnp.int32(1), nbits - 1 - s)
            c = count(lambda blk, pos, t, cd: jnp.logical_and(blk == t, pos < cd), [thr_t, cand])
            return jnp.where(c <= need, cand, p)
        return lax.fori_loop(0, nbits, pos_step, jnp.zeros((tq, 1), jnp.int32))

    p_cut = lax.cond(n_open(settled) > 0.0, tie_search, lambda: jnp.zeros((tq, 1), jnp.int32))
    p_cut = jnp.where(tie_rows, p_cut, 0)
    thr = jnp.where(tie_rows, thr_t, thr_ge - 1)

    def write_tile(kt, carry):
        off = pl.multiple_of(kt * tk, tk)
        blk = s_ref[:, pl.ds(off, tk)]
        keep = jnp.logical_or(blk > thr, jnp.logical_and(blk == thr, lane + off < p_cut))
        o_ref[0, :, pl.ds(off, tk)] = jnp.where(keep, 0.0, MASK_NEG).astype(BF16)
        return carry

    def blank_tile(kt, carry):
        off = pl.multiple_of(kt * tk, tk)
        o_ref[0, :, pl.ds(off, tk)] = jnp.full((tq, tk), MASK_NEG, BF16)
        return carry

    lax.fori_loop(0, n_kt, write_tile, 0)
    lax.fori_loop(n_kt, n_all, blank_tile, 0)


def _index_mask(qi, wi, ka, kb, *, q_pos0, l_valid, n_keep, tk):
    bsz, t, _ = qi.shape
    lp = ka.shape[1]
    tq = _pick(t, (128, 64, 32, 16))
    kern = functools.partial(_index_mask_kernel, tq=tq, tk=tk, cw=_pick(lp, (1024, 512, tk)), lp=lp,
                             q_pos0=q_pos0, l_valid=l_valid, n_keep=n_keep)
    return pl.pallas_call(
        kern,
        grid=(bsz, t // tq),
        in_specs=[pl.BlockSpec((1, tq, N_IDX_HEADS * IDX_DIM), lambda b, i: (b, i, 0)),
                  pl.BlockSpec((1, tq, N_IDX_HEADS), lambda b, i: (b, i, 0)),
                  pl.BlockSpec((1, lp, 2 * IDX_DIM), lambda b, i: (b, 0, 0)),
                  pl.BlockSpec((1, lp, 2 * IDX_DIM), lambda b, i: (b, 0, 0))],
        out_specs=pl.BlockSpec((1, tq, lp), lambda b, i: (b, i, 0)),
        out_shape=jax.ShapeDtypeStruct((bsz, t, lp), BF16),
        scratch_shapes=[pltpu.VMEM((tq, lp), jnp.int32), pltpu.VMEM((N_IDX_HEADS, tq, LANES), F32)],
        compiler_params=_cparams("parallel", "arbitrary"),
        name="index_mask",
    )(qi, wi, ka, kb)


_BUCKET_STARTS = tuple(int(math.ceil(8 * 16 ** ((b - 8) / 8) - 1e-9)) for b in range(9, 16))
FAR_BUCKET = N_BUCKETS // 2 - 1


def _bias_tile_kernel(delta_ref, tbl_ref, o_ref, *, tq, tk):
    case = pl.program_id(0)
    h = pl.program_id(1)
    rel = (delta_ref[case] + lax.broadcasted_iota(jnp.int32, (tq, tk), 1)
           - lax.broadcasted_iota(jnp.int32, (tq, tk), 0))
    n = jnp.abs(rel)
    large = jnp.full((tq, tk), 8, jnp.int32)
    for st in _BUCKET_STARTS:
        large = large + jnp.where(n >= st, 1, 0)
    bucket = jnp.where(rel > 0, N_BUCKETS // 2, 0) + jnp.where(n < 8, n, large)

    def body(b, acc):
        return jnp.where(bucket == b, tbl_ref[b * N_HEADS + h], acc)

    acc = lax.fori_loop(0, N_BUCKETS, body, jnp.zeros((tq, tk), F32))
    o_ref[0, 0] = (acc - tbl_ref[FAR_BUCKET * N_HEADS + h]) * LOG2E


def _bias_tiles(rel_bias, deltas, tq, tk):
    ncase = len(deltas)
    return pl.pallas_call(
        functools.partial(_bias_tile_kernel, tq=tq, tk=tk),
        grid_spec=pltpu.PrefetchScalarGridSpec(
            num_scalar_prefetch=2,
            grid=(ncase, N_HEADS),
            in_specs=[],
            out_specs=pl.BlockSpec((1, 1, tq, tk), lambda c, h, d, t: (c, h, 0, 0)),
        ),
        out_shape=jax.ShapeDtypeStruct((ncase, N_HEADS, tq, tk), F32),
        compiler_params=_cparams("arbitrary", "arbitrary"),
        name="bias_tiles",
    )(jnp.asarray(deltas, jnp.int32), rel_bias.reshape(-1))


def _attn_kernel(qb_ref, kt_ref, case_ref, first_ref, last_ref,
                 q_ref, k_ref, v_ref, mask_ref, bias_ref, o_ref, acc_ref, m_ref, l_ref,
                 s_ref, p_ref, a_ref, r_ref, mf_ref, *, tq, tk):
    s_id = pl.program_id(1)

    @pl.when(first_ref[s_id] == 1)
    def _():
        acc_ref[...] = jnp.zeros_like(acc_ref)
        m_ref[...] = jnp.full_like(m_ref, -jnp.inf)
        l_ref[...] = jnp.zeros_like(l_ref)

    rc = _pick(tq, (16, 8))
    mf_ref[...] = mask_ref[0].astype(F32)

    def run(with_bias):
        for g in range(N_KV_HEADS):
            h0 = g * GROUP
            qg = jnp.concatenate(
                [q_ref[0, :, (h0 + j) * HEAD_DIM:(h0 + j + 1) * HEAD_DIM] for j in range(GROUP)], axis=0)
            kg = k_ref[0, :, g * HEAD_DIM:(g + 1) * HEAD_DIM]
            vg = v_ref[0, :, g * HEAD_DIM:(g + 1) * HEAD_DIM]
            s_ref[...] = lax.dot_general(qg, kg, (((1,), (1,)), ((), ())), preferred_element_type=F32)
            nt = tk // LANES

            def logits(c, j):
                r0 = j * tq + c * rc
                sc = s_ref[r0:r0 + rc, :] + mf_ref[c * rc:(c + 1) * rc, :]
                if with_bias:
                    sc = sc + bias_ref[0, h0 + j, c * rc:(c + 1) * rc, :]
                return r0, sc

            for c in range(tq // rc):
                for j in range(GROUP):
                    r0, sc = logits(c, j)
                    mx = sc[:, 0:LANES]
                    for t in range(1, nt):
                        mx = jnp.maximum(mx, sc[:, t * LANES:(t + 1) * LANES])
                    r_ref[r0:r0 + rc, :] = mx
            m_prev = m_ref[h0:h0 + GROUP].reshape(GROUP * tq, LANES)
            m_new = jnp.maximum(m_prev, jnp.max(r_ref[...], axis=-1, keepdims=True))
            a_ref[...] = jnp.exp2(m_prev - m_new)
            m_ref[h0:h0 + GROUP] = m_new.reshape(GROUP, tq, LANES)
            for c in range(tq // rc):
                for j in range(GROUP):
                    r0, sc = logits(c, j)
                    mb = m_ref[h0 + j, c * rc:(c + 1) * rc, :]
                    p = jnp.exp2(sc - jnp.concatenate([mb] * nt, axis=1))
                    ps = p[:, 0:LANES]
                    for t in range(1, nt):
                        ps = ps + p[:, t * LANES:(t + 1) * LANES]
                    r_ref[r0:r0 + rc, :] = ps
                    p_ref[r0:r0 + rc, :] = p.astype(BF16)
            alpha = a_ref[...].reshape(GROUP, tq, LANES)
            row_sum = jnp.sum(r_ref[...], axis=-1, keepdims=True).reshape(GROUP, tq, 1)
            l_ref[h0:h0 + GROUP] = alpha * l_ref[h0:h0 + GROUP] + row_sum
            pv = jnp.dot(p_ref[...], vg, preferred_element_type=F32)
            acc_ref[h0:h0 + GROUP] = alpha * acc_ref[h0:h0 + GROUP] + pv.reshape(GROUP, tq, HEAD_DIM)

    lax.cond(case_ref[s_id] >= 0, lambda: run(True), lambda: run(False))

    @pl.when(last_ref[s_id] == 1)
    def _():
        for h in range(N_HEADS):
            o_ref[0, :, h * HEAD_DIM:(h + 1) * HEAD_DIM] = (acc_ref[h] / l_ref[h]).astype(BF16)


def _attention(q, k, v, mask, rel_bias, *, q_pos0, l_valid):
    bsz, t, _ = q.shape
    lp = k.shape[1]
    tq = _pick(t, (256, 128, 64, 32, 16))
    tk = _pick(lp, (512, 256, 128))
    qb, kt, delta, first, last = [], [], [], [], []
    for i in range(t // tq):
        qlast = q_pos0 + i * tq + tq - 1
        n_kt = -(-min((qlast // CHUNK + 1) * CHUNK, l_valid) // tk)
        for j in range(n_kt):
            qb.append(i)
            kt.append(j)
            delta.append(j * tk - (q_pos0 + i * tq))
            first.append(int(j == 0))
            last.append(int(j == n_kt - 1))
    need = [d + tk - 1 > -LANES for d in delta]
    deltas = sorted({d for d, nd in zip(delta, need) if nd})
    case = [deltas.index(d) if nd else -1 for d, nd in zip(delta, need)]
    fetch, cur = [], 0
    for cs in case:
        cur = cs if cs >= 0 else cur
        fetch.append(cur)
    bias = _bias_tiles(rel_bias, deltas, tq, tk)
    tabs = [jnp.asarray(a, jnp.int32) for a in (qb, kt, case, first, last, fetch)]
    d = N_HEADS * HEAD_DIM
    dkv = N_KV_HEADS * HEAD_DIM

    def kern(qb_ref, kt_ref, case_ref, first_ref, last_ref, fetch_ref, *refs):
        _attn_kernel(qb_ref, kt_ref, case_ref, first_ref, last_ref, *refs, tq=tq, tk=tk)

    return pl.pallas_call(
        kern,
        grid_spec=pltpu.PrefetchScalarGridSpec(
            num_scalar_prefetch=6,
            grid=(bsz, len(qb)),
            in_specs=[pl.BlockSpec((1, tq, d), lambda b, s, qb, kt, cs, fi, la, fe: (b, qb[s], 0)),
                      pl.BlockSpec((1, tk, dkv), lambda b, s, qb, kt, cs, fi, la, fe: (b, kt[s], 0)),
                      pl.BlockSpec((1, tk, dkv), lambda b, s, qb, kt, cs, fi, la, fe: (b, kt[s], 0)),
                      pl.BlockSpec((1, tq, tk), lambda b, s, qb, kt, cs, fi, la, fe: (b, qb[s], kt[s])),
                      pl.BlockSpec((1, N_HEADS, tq, tk), lambda b, s, qb, kt, cs, fi, la, fe: (fe[s], 0, 0, 0))],
            out_specs=pl.BlockSpec((1, tq, d), lambda b, s, qb, kt, cs, fi, la, fe: (b, qb[s], 0)),
            scratch_shapes=[pltpu.VMEM((N_HEADS, tq, HEAD_DIM), F32),
                            pltpu.VMEM((N_HEADS, tq, LANES), F32),
                            pltpu.VMEM((N_HEADS, tq, LANES), F32),
                            pltpu.VMEM((GROUP * tq, tk), F32),
                            pltpu.VMEM((GROUP * tq, tk), BF16),
                            pltpu.VMEM((GROUP * tq, LANES), F32),
                            pltpu.VMEM((GROUP * tq, LANES), F32),
                            pltpu.VMEM((tq, tk), F32)],
        ),
        out_shape=jax.ShapeDtypeStruct((bsz, t, d), BF16),
        compiler_params=_cparams("parallel", "arbitrary"),
        name="attention",
    )(*tabs, q, k, v, mask, bias)


def _pad_rows(a, lp):
    return jnp.pad(a, ((0, 0), (0, lp - a.shape[1]), (0, 0)))


def _dsa_layer(h, past, past_len, rel_bias, gain, wq, wkv, widx, wo):
    bsz, t, d = h.shape
    hf = h.reshape(bsz * t, d)
    dkv = N_KV_HEADS * HEAD_DIM
    q = _norm_matmul(hf, gain, wq, BF16, scale=HEAD_DIM ** -0.5 * LOG2E).reshape(bsz, t, d)
    kv = _norm_matmul(hf, gain, wkv, F32).reshape(bsz, t, 2 * dkv)
    idx = _norm_matmul(hf, gain, widx, F32).reshape(bsz, t, -1)
    k, v = kv[..., :dkv], kv[..., dkv:]
    nqi = N_IDX_HEADS * IDX_DIM
    qi = idx[..., :nqi].astype(BF16)
    ki = idx[..., nqi:nqi + IDX_DIM]
    wi = idx[..., nqi + IDX_DIM:nqi + IDX_DIM + N_IDX_HEADS] * IDX_SCALE
    if past is None:
        k_all, v_all, ki_all = k, v, ki
    else:
        k_all = jnp.concatenate([past[0].reshape(bsz, past_len, dkv), k], axis=1)
        v_all = jnp.concatenate([past[1].reshape(bsz, past_len, dkv), v], axis=1)
        ki_all = jnp.concatenate([past[2], ki], axis=1)
    l_valid = past_len + t
    n_keep = min(TOPK_MAX, l_valid // 4)
    tk = 512 if l_valid % 512 == 0 else 256
    lp = -(-l_valid // tk) * tk
    kb16 = _pad_rows(k_all.astype(BF16), lp)
    vb16 = _pad_rows(v_all.astype(BF16), lp)
    ki16 = _pad_rows(ki_all.astype(BF16), lp)
    zeros = jnp.zeros_like(ki16)
    ka = jnp.concatenate([ki16, zeros], axis=-1)
    kb = jnp.concatenate([zeros, ki16], axis=-1)
    mask = _index_mask(qi, wi, ka, kb, q_pos0=past_len, l_valid=l_valid, n_keep=n_keep, tk=256)
    o = _attention(q, kb16, vb16, mask, rel_bias, q_pos0=past_len, l_valid=l_valid)
    hout = _matmul_residual(o.reshape(bsz * t, d), wo, jnp.zeros((d,), F32), hf).reshape(bsz, t, d)
    return hout, k, v, ki


def _conv_layer(h, past, gain, w_pw1, b_pw1, w_dw, b_dw, ln_g, ln_b, w_pw2, b_pw2):
    bsz, t, d = h.shape
    hf = h.reshape(bsz * t, d)
    u = _norm_glu(hf, gain, w_pw1, b_pw1).reshape(bsz, t, d)
    if past is None:
        past = jnp.zeros((bsz, CONV_STATE, d), F32)
    past32 = jnp.concatenate([jnp.zeros((bsz, HALO - CONV_STATE, d), F32), past], axis=1)
    z = _dwconv_ln_swish(u, past32, w_dw, b_dw, ln_g, ln_b)
    hout = _matmul_residual(z.reshape(bsz * t, d), w_pw2, b_pw2, hf).reshape(bsz, t, d)
    state = jnp.concatenate([past, u], axis=1)[:, -CONV_STATE:]
    return hout, state


def _trunk(x, past, past_len, p):
    bsz, t, d = x.shape
    h, k, v, ki = _dsa_layer(x, None if past is None else past[:3], past_len, p["rel_bias"],
                             p["norm_mix"][0], p["wq"], p["wkv"], p["widx"], p["wo"])
    h = _ffn(h.reshape(bsz * t, d), p["norm_ffn"][0], p["wg"][0], p["wu"][0], p["wd"][0],
             p["norm_final"], False).reshape(bsz, t, d)
    h, cs = _conv_layer(h, None if past is None else past[3], p["norm_mix"][1], p["w_pw1"], p["b_pw1"],
                        p["w_dw"], p["b_dw"], p["ln_g"], p["ln_b"], p["w_pw2"], p["b_pw2"])
    y = _ffn(h.reshape(bsz * t, d), p["norm_ffn"][1], p["wg"][1], p["wu"][1], p["wd"][1],
             p["norm_final"], True).reshape(bsz, t, d)
    return (y, k.reshape(1, bsz, t, N_KV_HEADS, HEAD_DIM), v.reshape(1, bsz, t, N_KV_HEADS, HEAD_DIM),
            ki[None], cs[None])


def kernel(x_prompt, x_sample, cache_k, cache_v, cache_kidx, state_conv, rel_bias, norm_mix, norm_ffn, norm_final, attn_wq, attn_wk, attn_wv, attn_wo, idx_wq, idx_wk, idx_ww, conv_w_pw1, conv_b_pw1, conv_w_dw, conv_b_dw, conv_ln_g, conv_ln_b, conv_w_pw2, conv_b_pw2, ffn_w_gate, ffn_w_up, ffn_w_down):
    d = x_prompt.shape[-1]
    nidx = N_IDX_HEADS * IDX_DIM + IDX_DIM + N_IDX_HEADS
    nidx_pad = -(-nidx // LANES) * LANES
    widx = jnp.concatenate([idx_wq[0], idx_wk[0], idx_ww[0], jnp.zeros((d, nidx_pad - nidx), F32)], axis=1)
    p = dict(
        rel_bias=rel_bias, norm_mix=norm_mix, norm_ffn=norm_ffn, norm_final=norm_final,
        wq=attn_wq[0].astype(BF16),
        wkv=jnp.concatenate([attn_wk[0], attn_wv[0]], axis=1).astype(BF16),
        widx=widx.astype(BF16),
        wo=attn_wo[0].astype(BF16),
        wg=ffn_w_gate.astype(BF16), wu=ffn_w_up.astype(BF16), wd=ffn_w_down.astype(BF16),
        w_pw1=conv_w_pw1[0].astype(BF16), b_pw1=conv_b_pw1[0], w_dw=conv_w_dw[0], b_dw=conv_b_dw[0],
        ln_g=conv_ln_g[0], ln_b=conv_ln_b[0], w_pw2=conv_w_pw2[0].astype(BF16), b_pw2=conv_b_pw2[0],
    )
    y_p, k_p, v_p, ki_p, cs_p = _trunk(x_prompt, None, 0, p)
    past = (cache_k[0], cache_v[0], cache_kidx[0], state_conv[0])
    y_s, k_s, v_s, ki_s, cs_s = _trunk(x_sample, past, cache_k.shape[2], p)
    return (y_p, y_s, k_p, v_p, ki_p, cs_p, k_s, v_s, ki_s, cs_s)
```

```python
import functools
import math

import numpy as np
import jax
import jax.numpy as jnp
from jax import lax
from jax.experimental import pallas as pl
from jax.experimental.pallas import tpu as pltpu

CHUNK = 64
N_HEADS = 16
N_KV_HEADS = 4
GROUP = N_HEADS // N_KV_HEADS
HEAD_DIM = 128
N_IDX_HEADS = 16
IDX_DIM = 64
IDX_SCALE = (N_IDX_HEADS ** -0.5) * (IDX_DIM ** -0.5)
TOPK_MAX = 256
N_BUCKETS = 32
DW_WIDTH = 31
CONV_STATE = DW_WIDTH - 1
RMS_EPS = 1e-6
LN_EPS = 1e-5

LANES = 128
SUBLANES = 8
HALO = 32
MASK_NEG = -1e30
LOG2E = math.log2(math.e)
INT_MIN = -(2 ** 31)
VMEM_LIMIT = 48 * 1024 * 1024

F32 = jnp.float32
BF16 = jnp.bfloat16


def _cparams(*sem):
    return pltpu.CompilerParams(dimension_semantics=sem, vmem_limit_bytes=VMEM_LIMIT)


def _pick(n, prefs):
    for p in prefs:
        if n % p == 0:
            return p
    return n


def _rms(x, g, eps):
    return x * lax.rsqrt(jnp.mean(x * x, axis=-1, keepdims=True) + eps) * g


def _sigmoid(x):
    return 1.0 / (1.0 + jnp.exp(-x))


def _rows_kernel(*refs, norm, has_bias, has_res, segs, rs, tn):
    refs = list(refs)
    x_ref = refs.pop(0)
    g_ref = refs.pop(0) if norm else None
    w_ref = refs.pop(0)
    b_ref = refs.pop(0) if has_bias else None
    r_ref = refs.pop(0) if has_res else None
    outs = refs
    tm = x_ref.shape[0]
    for r0 in range(0, tm, rs):
        xa = x_ref[r0:r0 + rs, :]
        hn = _rms(xa, g_ref[...], RMS_EPS).astype(BF16) if norm else xa
        for col0, width, gate0, scale, seg_outs in segs:
            for c in range(0, width, tn):
                cw = min(tn, width - c)
                y = jnp.dot(hn, w_ref[:, col0 + c:col0 + c + cw], preferred_element_type=F32)
                if has_bias:
                    y = y + b_ref[:, col0 + c:col0 + c + cw]
                if gate0 is not None:
                    gt = jnp.dot(hn, w_ref[:, gate0 + c:gate0 + c + cw], preferred_element_type=F32)
                    if has_bias:
                        gt = gt + b_ref[:, gate0 + c:gate0 + c + cw]
                    y = y * _sigmoid(gt)
                if scale != 1.0:
                    y = y * scale
                if has_res:
                    y = y + r_ref[r0:r0 + rs, c:c + cw]
                for oi, ow in seg_outs:
                    if c < ow:
                        n_c = min(cw, ow - c)
                        outs[oi][r0:r0 + rs, c:c + n_c] = y[:, :n_c].astype(outs[oi].dtype)


def _rows_matmul(x, w, segs, out_defs, *, gain=None, bias=None, res=None, name):
    n, k = x.shape
    cols = w.shape[1]
    tm = _pick(n, (512, 256, 128, 64, 32, 16, 8))
    rs = _pick(tm, (256, 128, 64, 32, 16, 8))
    once = dict(pipeline_mode=pl.Buffered(1))
    args, in_specs = [x], [pl.BlockSpec((tm, k), lambda i: (i, 0))]
    if gain is not None:
        args.append(gain.reshape(1, k))
        in_specs.append(pl.BlockSpec((1, k), lambda i: (0, 0), **once))
    args.append(w)
    in_specs.append(pl.BlockSpec((k, cols), lambda i: (0, 0), **once))
    if bias is not None:
        args.append(bias.reshape(1, cols))
        in_specs.append(pl.BlockSpec((1, cols), lambda i: (0, 0), **once))
    if res is not None:
        args.append(res)
        in_specs.append(pl.BlockSpec((tm, res.shape[1]), lambda i: (i, 0)))
    kern = functools.partial(_rows_kernel, norm=gain is not None, has_bias=bias is not None,
                             has_res=res is not None, segs=segs, rs=rs, tn=512)
    return pl.pallas_call(
        kern,
        grid=(n // tm,),
        in_specs=in_specs,
        out_specs=[pl.BlockSpec((tm, ow), lambda i: (i, 0)) for ow, _ in out_defs],
        out_shape=[jax.ShapeDtypeStruct((n, ow), dt) for ow, dt in out_defs],
        compiler_params=_cparams("parallel"),
        name=name,
    )(*args)


def _ffn_kernel(x_ref, g_ref, wg_ref, wu_ref, wd_ref, gf_ref, o_ref, hn_ref, acc_ref, *, final_norm):
    j = pl.program_id(1)

    @pl.when(j == 0)
    def _():
        hn_ref[...] = _rms(x_ref[...], g_ref[...], RMS_EPS).astype(BF16)
        acc_ref[...] = jnp.zeros_like(acc_ref)

    hn = hn_ref[...]
    a = jnp.dot(hn, wg_ref[...], preferred_element_type=F32)
    u = jnp.dot(hn, wu_ref[...], preferred_element_type=F32)
    act = (a * _sigmoid(a)) * u
    acc_ref[...] += jnp.dot(act.astype(BF16), wd_ref[...], preferred_element_type=F32)

    @pl.when(j == pl.num_programs(1) - 1)
    def _():
        y = x_ref[...] + acc_ref[...]
        if final_norm:
            y = _rms(y, gf_ref[...], RMS_EPS)
        o_ref[...] = y


def _ffn(x, gain, wg, wu, wd, gain_final, final_norm):
    n, d = x.shape
    f = wg.shape[1]
    tm = _pick(n, (512, 256, 128, 64, 32, 16, 8))
    tf = _pick(f, (512, 256, 128))
    return pl.pallas_call(
        functools.partial(_ffn_kernel, final_norm=final_norm),
        grid=(n // tm, f // tf),
        in_specs=[pl.BlockSpec((tm, d), lambda i, j: (i, 0)),
                  pl.BlockSpec((1, d), lambda i, j: (0, 0)),
                  pl.BlockSpec((d, tf), lambda i, j: (0, j)),
                  pl.BlockSpec((d, tf), lambda i, j: (0, j)),
                  pl.BlockSpec((tf, d), lambda i, j: (j, 0)),
                  pl.BlockSpec((1, d), lambda i, j: (0, 0))],
        out_specs=pl.BlockSpec((tm, d), lambda i, j: (i, 0)),
        out_shape=jax.ShapeDtypeStruct((n, d), F32),
        scratch_shapes=[pltpu.VMEM((tm, d), BF16), pltpu.VMEM((tm, d), F32)],
        compiler_params=_cparams("parallel", "arbitrary"),
        name="ffn",
    )(x, gain.reshape(1, d), wg, wu, wd, gain_final.reshape(1, d))


def _dwconv_kernel(u_ref, halo_ref, past_ref, w_ref, b_ref, lg_ref, lb_ref, o_ref, win_ref, dw_ref, sh_ref,
                   *, tt, c):
    halo = jnp.where(pl.program_id(1) == 0, past_ref[0], halo_ref[0])
    win_ref[0:HALO, :] = halo
    win_ref[HALO:HALO + tt, :] = u_ref[0]

    span = tt + HALO - SUBLANES

    def chunk(ci, carry):
        off = pl.multiple_of(ci * LANES, LANES)
        for s in range(1, SUBLANES):
            sh_ref[s - 1] = win_ref[pl.ds(s, span), pl.ds(off, LANES)]
        acc = jnp.zeros((tt, LANES), F32) + b_ref[:, pl.ds(off, LANES)]
        for k in range(DW_WIDTH):
            r = k + HALO - CONV_STATE
            s, base = r % SUBLANES, r - r % SUBLANES
            rows = win_ref[pl.ds(base, tt), pl.ds(off, LANES)] if s == 0 else sh_ref[s - 1, base:base + tt, :]
            acc = acc + w_ref[k:k + 1, pl.ds(off, LANES)] * rows
        dw_ref[:, pl.ds(off, LANES)] = acc
        return carry

    lax.fori_loop(0, c // LANES, chunk, 0)
    df = dw_ref[...]
    mu = jnp.mean(df, axis=-1, keepdims=True)
    var = jnp.mean(jnp.square(df - mu), axis=-1, keepdims=True)
    z = (df - mu) * lax.rsqrt(var + LN_EPS) * lg_ref[...] + lb_ref[...]
    o_ref[0] = (z * _sigmoid(z)).astype(BF16)


def _dwconv_ln_swish(u, past32, w_dw, b_dw, ln_g, ln_b):
    bsz, t, c = u.shape
    tt = _pick(t, (256, 128, 64, 32))
    w32 = jnp.concatenate([w_dw, jnp.zeros((HALO - DW_WIDTH, c), F32)], axis=0)
    r = tt // HALO
    return pl.pallas_call(
        functools.partial(_dwconv_kernel, tt=tt, c=c),
        grid=(bsz, t // tt),
        in_specs=[pl.BlockSpec((1, tt, c), lambda b, i: (b, i, 0)),
                  pl.BlockSpec((1, HALO, c), lambda b, i: (b, jnp.maximum(i * r - 1, 0), 0)),
                  pl.BlockSpec((1, HALO, c), lambda b, i: (b, 0, 0)),
                  pl.BlockSpec((HALO, c), lambda b, i: (0, 0)),
                  pl.BlockSpec((1, c), lambda b, i: (0, 0)),
                  pl.BlockSpec((1, c), lambda b, i: (0, 0)),
                  pl.BlockSpec((1, c), lambda b, i: (0, 0))],
        out_specs=pl.BlockSpec((1, tt, c), lambda b, i: (b, i, 0)),
        out_shape=jax.ShapeDtypeStruct((bsz, t, c), BF16),
        scratch_shapes=[pltpu.VMEM((tt + HALO, c), F32), pltpu.VMEM((tt, c), F32),
                        pltpu.VMEM((SUBLANES - 1, tt + HALO - SUBLANES, LANES), F32)],
        compiler_params=_cparams("parallel", "arbitrary"),
        name="dwconv_ln_swish",
    )(u, u, past32, w32, b_dw.reshape(1, c), ln_g.reshape(1, c), ln_b.reshape(1, c))


def _adm_end(qpos, l_valid):
    return jnp.minimum((qpos // CHUNK + 1) * CHUNK, l_valid)


def _flip_magnitude(bits):
    return bits ^ ((bits >> 31) & jnp.int32(0x7FFFFFFF))


def _sort_key(x):
    return _flip_magnitude(lax.bitcast_convert_type(x, jnp.int32))


def _key_value(key):
    return lax.bitcast_convert_type(_flip_magnitude(key), F32)


def _index_mask_kernel(qi_ref, wt_ref, ka_ref, kb_ref, o_ref, s_ref, g_ref, *,
                       tq, tk, cw, lp, q_pos0, l_valid, n_keep):
    i = pl.program_id(1)
    q0 = q_pos0 + i * tq
    n_kt = (_adm_end(q0 + tq - 1, l_valid) + tk - 1) // tk
    n_all = lp // tk
    col_end = _adm_end(q0 + lax.broadcasted_iota(jnp.int32, (1, tq), 1), l_valid)
    ksub = 32
    kst = 128

    def score_tile(kt, carry):
        for st in range(tk // kst):
            off = pl.multiple_of(kt * tk + st * kst, kst)
            ka = ka_ref[0, pl.ds(off, kst), :]
            kb = kb_ref[0, pl.ds(off, kst), :]
            acc = jnp.zeros((kst, tq), F32)
            for p in range(N_IDX_HEADS // 2):
                q2 = qi_ref[0, :, 2 * IDX_DIM * p:2 * IDX_DIM * (p + 1)]
                for half, kk in ((0, ka), (1, kb)):
                    h = 2 * p + half
                    d = lax.dot_general(kk, q2, (((1,), (1,)), ((), ())), preferred_element_type=F32)
                    acc = acc + wt_ref[0, h:h + 1, :] * jnp.maximum(d, 0.0)
            adm = off + lax.broadcasted_iota(jnp.int32, (kst, tq), 0) < col_end
            s_ref[pl.ds(off, kst), :] = jnp.where(adm, _sort_key(acc), jnp.int32(INT_MIN))
            gsl = slice(st * kst, (st + 1) * kst)
            g_ref[gsl, :] = jnp.maximum(g_ref[gsl, :], jnp.where(adm, acc, -jnp.inf))
        return carry

    g_ref[...] = jnp.full((tk, tq), -jnp.inf, F32)

    def pad_tile(kt, carry):
        s_ref[pl.ds(pl.multiple_of(kt * tk, tk), tk), :] = jnp.full((tk, tq), INT_MIN, jnp.int32)
        return carry

    n_ct = (n_kt * tk + cw - 1) // cw
    lax.fori_loop(0, n_kt, score_tile, 0)
    lax.fori_loop(n_kt, n_ct * (cw // tk), pad_tile, 0)

    def count(pred, args):
        sub = lax.broadcasted_iota(jnp.int32, (ksub, tq), 0)

        def body(ct, acc):
            for r in range(cw // ksub):
                off = pl.multiple_of(ct * cw + r * ksub, ksub)
                acc = acc + jnp.where(pred(s_ref[pl.ds(off, ksub), :], sub + off, *args), 1.0, 0.0)
            return acc
        acc = lax.fori_loop(0, n_ct, body, jnp.zeros((ksub, tq), F32))
        return jnp.sum(acc, axis=0, keepdims=True)

    kf = float(n_keep)
    probes_per_check = 2

    g = g_ref[...]
    k_lo = _sort_key(jnp.min(g, axis=0, keepdims=True))
    k_hi = _sort_key(jnp.max(g, axis=0, keepdims=True))
    x = k_lo ^ k_hi
    xf_bits = lax.bitcast_convert_type(jnp.maximum(x, 1).astype(F32), jnp.int32)
    b0 = jnp.where(x < 0, 31, jnp.minimum((xf_bits >> 23) - 127, 31))
    if tk < n_keep:
        b0 = jnp.full((1, tq), 31, jnp.int32)
    top = lax.shift_left(jnp.int32(1), b0)
    u0 = (k_lo ^ jnp.int32(INT_MIN)) & ~(top | (top - 1))

    def probe(step, u, settled, thr_ge):
        b = b0 - step
        cand = jnp.where(b >= 0, u | lax.shift_left(jnp.int32(1), jnp.maximum(b, 0)), u)
        ct = cand ^ jnp.int32(INT_MIN)
        c = count(lambda blk, pos, t: blk >= t, [ct])
        exact = jnp.logical_and(c == kf, settled == 0)
        return (jnp.where(c >= kf, cand, u), jnp.where(exact, 1, settled), jnp.where(exact, ct, thr_ge))

    def n_open(settled, step):
        return jnp.max(jnp.where(jnp.logical_and(settled == 0, b0 - step >= 0), 1.0, 0.0))

    def probe_group(state):
        g, u, settled, thr_ge, _ = state
        for j in range(probes_per_check):
            u, settled, thr_ge = probe(g * probes_per_check + j, u, settled, thr_ge)
        return g + 1, u, settled, thr_ge, n_open(settled, (g + 1) * probes_per_check)

    settled0 = jnp.where(col_end <= n_keep, 1, 0)
    state0 = (jnp.int32(0), u0, settled0,
              jnp.full((1, tq), INT_MIN + 1, jnp.int32), n_open(settled0, 0))
    _, u, settled, thr_ge, _ = lax.while_loop(
        lambda st: jnp.logical_and(st[4] > 0.0, st[0] < 32 // probes_per_check), probe_group, state0)

    tie_rows = settled == 0
    thr_t = u ^ jnp.int32(INT_MIN)
    nbits = max(1, int(lp).bit_length())

    def tie_search():
        need = kf - count(lambda blk, pos, t: blk > t, [thr_t])

        def pos_step(s, p):
            cand = p | lax.shift_left(jnp.int32(1), nbits - 1 - s)
            c = count(lambda blk, pos, t, cd: jnp.logical_and(blk == t, pos < cd), [thr_t, cand])
            return jnp.where(c <= need, cand, p)
        return lax.fori_loop(0, nbits, pos_step, jnp.zeros((1, tq), jnp.int32))

    any_tie = jnp.max(jnp.where(tie_rows, 1.0, 0.0)) > 0.0
    p_cut = lax.cond(any_tie, tie_search, lambda: jnp.zeros((1, tq), jnp.int32))
    p_cut = jnp.where(tie_rows, p_cut, 0)
    thr = jnp.where(tie_rows, thr_t, thr_ge - 1)

    def write_tile(kt, carry):
        for st in range(tk // kst):
            off = pl.multiple_of(kt * tk + st * kst, kst)
            blk = s_ref[pl.ds(off, kst), :]
            kpos = off + lax.broadcasted_iota(jnp.int32, (kst, tq), 0)
            keep = jnp.logical_or(blk > thr, jnp.logical_and(blk == thr, kpos < p_cut))
            o_ref[0, :, pl.ds(off, kst)] = jnp.where(keep, 0.0, MASK_NEG).T.astype(BF16)
        return carry

    def blank_tile(kt, carry):
        off = pl.multiple_of(kt * tk, tk)
        o_ref[0, :, pl.ds(off, tk)] = jnp.full((tq, tk), MASK_NEG, BF16)
        return carry

    lax.fori_loop(0, n_kt, write_tile, 0)
    lax.fori_loop(n_kt, n_all, blank_tile, 0)


def _index_mask(qi, wi, ka, kb, *, q_pos0, l_valid, n_keep):
    bsz, t, _ = qi.shape
    lp = ka.shape[1]
    tp = -(-t // LANES) * LANES
    if tp != t:
        qi = jnp.pad(qi, ((0, 0), (0, tp - t), (0, 0)))
        wi = jnp.pad(wi, ((0, 0), (0, tp - t), (0, 0)))
    wt = jnp.swapaxes(wi, 1, 2)
    tq = _pick(tp, (256, 128))
    tk = 256
    kern = functools.partial(_index_mask_kernel, tq=tq, tk=tk, cw=tk, lp=lp,
                             q_pos0=q_pos0, l_valid=l_valid, n_keep=n_keep)
    mask = pl.pallas_call(
        kern,
        grid=(bsz, tp // tq),
        in_specs=[pl.BlockSpec((1, tq, N_IDX_HEADS * IDX_DIM), lambda b, i: (b, i, 0)),
                  pl.BlockSpec((1, N_IDX_HEADS, tq), lambda b, i: (b, 0, i)),
                  pl.BlockSpec((1, lp, 2 * IDX_DIM), lambda b, i: (b, 0, 0)),
                  pl.BlockSpec((1, lp, 2 * IDX_DIM), lambda b, i: (b, 0, 0))],
        out_specs=pl.BlockSpec((1, tq, lp), lambda b, i: (b, i, 0)),
        out_shape=jax.ShapeDtypeStruct((bsz, tp, lp), BF16),
        scratch_shapes=[pltpu.VMEM((lp, tq), jnp.int32), pltpu.VMEM((tk, tq), F32)],
        compiler_params=_cparams("parallel", "arbitrary"),
        name="index_mask",
    )(qi, wt, ka, kb)
    return mask[:, :t]


_BUCKET_STARTS = tuple(int(math.ceil(8 * 16 ** ((b - 8) / 8) - 1e-9)) for b in range(9, 16))
FAR_BUCKET = N_BUCKETS // 2 - 1


def _bias_tile_kernel(delta_ref, tbl_ref, o_ref, *, tq, tk):
    case = pl.program_id(0)
    h = pl.program_id(1)
    rel = (delta_ref[case] + lax.broadcasted_iota(jnp.int32, (tq, tk), 1)
           - lax.broadcasted_iota(jnp.int32, (tq, tk), 0))
    n = jnp.abs(rel)
    large = jnp.full((tq, tk), 8, jnp.int32)
    for st in _BUCKET_STARTS:
        large = large + jnp.where(n >= st, 1, 0)
    bucket = jnp.where(rel > 0, N_BUCKETS // 2, 0) + jnp.where(n < 8, n, large)

    def body(b, acc):
        return jnp.where(bucket == b, tbl_ref[b * N_HEADS + h], acc)

    acc = lax.fori_loop(0, N_BUCKETS, body, jnp.zeros((tq, tk), F32))
    o_ref[0, 0] = (acc - tbl_ref[FAR_BUCKET * N_HEADS + h]) * LOG2E


def _bias_tiles(rel_bias, deltas, tq, tk):
    ncase = len(deltas)
    return pl.pallas_call(
        functools.partial(_bias_tile_kernel, tq=tq, tk=tk),
        grid_spec=pltpu.PrefetchScalarGridSpec(
            num_scalar_prefetch=2,
            grid=(ncase, N_HEADS),
            in_specs=[],
            out_specs=pl.BlockSpec((1, 1, tq, tk), lambda c, h, d, t: (c, h, 0, 0)),
        ),
        out_shape=jax.ShapeDtypeStruct((ncase, N_HEADS, tq, tk), F32),
        compiler_params=_cparams("arbitrary", "arbitrary"),
        name="bias_tiles",
    )(jnp.asarray(deltas, jnp.int32), rel_bias.reshape(-1))


def _attn_kernel(qb_ref, kt_ref, case_ref, first_ref, last_ref,
                 q_ref, k_ref, v_ref, mask_ref, bias_ref, o_ref, acc_ref, m_ref, l_ref,
                 s_ref, p_ref, a_ref, r_ref, mf_ref, *, tq, tk):
    s_id = pl.program_id(1)

    @pl.when(first_ref[s_id] == 1)
    def _():
        acc_ref[...] = jnp.zeros_like(acc_ref)
        m_ref[...] = jnp.full_like(m_ref, -jnp.inf)
        l_ref[...] = jnp.zeros_like(l_ref)

    rc = _pick(tq, (16, 8))
    mf_ref[...] = mask_ref[0].astype(F32)

    def run(with_bias):
        for g in range(N_KV_HEADS):
            h0 = g * GROUP
            qg = jnp.concatenate(
                [q_ref[0, :, (h0 + j) * HEAD_DIM:(h0 + j + 1) * HEAD_DIM] for j in range(GROUP)], axis=0)
            kg = k_ref[0, :, g * HEAD_DIM:(g + 1) * HEAD_DIM]
            vg = v_ref[0, :, g * HEAD_DIM:(g + 1) * HEAD_DIM]
            s_ref[...] = lax.dot_general(qg, kg, (((1,), (1,)), ((), ())), preferred_element_type=F32)
            nt = tk // LANES

            def logits(c, j):
                r0 = j * tq + c * rc
                sc = s_ref[r0:r0 + rc, :] + mf_ref[c * rc:(c + 1) * rc, :]
                if with_bias:
                    sc = sc + bias_ref[0, h0 + j, c * rc:(c + 1) * rc, :]
                return r0, sc

            for c in range(tq // rc):
                for j in range(GROUP):
                    r0, sc = logits(c, j)
                    mx = sc[:, 0:LANES]
                    for t in range(1, nt):
                        mx = jnp.maximum(mx, sc[:, t * LANES:(t + 1) * LANES])
                    r_ref[r0:r0 + rc, :] = mx
            m_prev = m_ref[h0:h0 + GROUP].reshape(GROUP * tq, LANES)
            m_new = jnp.maximum(m_prev, jnp.max(r_ref[...], axis=-1, keepdims=True))
            a_ref[...] = jnp.exp2(m_prev - m_new)
            m_ref[h0:h0 + GROUP] = m_new.reshape(GROUP, tq, LANES)
            for c in range(tq // rc):
                for j in range(GROUP):
                    r0, sc = logits(c, j)
                    mb = m_ref[h0 + j, c * rc:(c + 1) * rc, :]
                    p = jnp.exp2(sc - jnp.concatenate([mb] * nt, axis=1))
                    ps = p[:, 0:LANES]
                    for t in range(1, nt):
                        ps = ps + p[:, t * LANES:(t + 1) * LANES]
                    r_ref[r0:r0 + rc, :] = ps
                    p_ref[r0:r0 + rc, :] = p.astype(BF16)
            alpha = a_ref[...].reshape(GROUP, tq, LANES)
            row_sum = jnp.sum(r_ref[...], axis=-1, keepdims=True).reshape(GROUP, tq, 1)
            l_ref[h0:h0 + GROUP] = alpha * l_ref[h0:h0 + GROUP] + row_sum
            pv = jnp.dot(p_ref[...], vg, preferred_element_type=F32)
            acc_ref[h0:h0 + GROUP] = alpha * acc_ref[h0:h0 + GROUP] + pv.reshape(GROUP, tq, HEAD_DIM)

    lax.cond(case_ref[s_id] >= 0, lambda: run(True), lambda: run(False))

    @pl.when(last_ref[s_id] == 1)
    def _():
        for h in range(N_HEADS):
            o_ref[0, :, h * HEAD_DIM:(h + 1) * HEAD_DIM] = (acc_ref[h] / l_ref[h]).astype(BF16)


def _attention(q, k, v, mask, rel_bias, *, q_pos0, l_valid):
    bsz, t, _ = q.shape
    lp = k.shape[1]
    tq = _pick(t, (256, 128, 64, 32, 16))
    tk = _pick(lp, (512, 256, 128))
    qb, kt, delta, first, last = [], [], [], [], []
    for i in range(t // tq):
        qlast = q_pos0 + i * tq + tq - 1
        n_kt = -(-min((qlast // CHUNK + 1) * CHUNK, l_valid) // tk)
        for j in range(n_kt):
            qb.append(i)
            kt.append(j)
            delta.append(j * tk - (q_pos0 + i * tq))
            first.append(int(j == 0))
            last.append(int(j == n_kt - 1))
    need = [d + tk - 1 > -LANES for d in delta]
    deltas = sorted({d for d, nd in zip(delta, need) if nd})
    case = [deltas.index(d) if nd else -1 for d, nd in zip(delta, need)]
    fetch, cur = [], 0
    for cs in case:
        cur = cs if cs >= 0 else cur
        fetch.append(cur)
    bias = _bias_tiles(rel_bias, deltas, tq, tk)
    tabs = [jnp.asarray(a, jnp.int32) for a in (qb, kt, case, first, last, fetch)]
    d = N_HEADS * HEAD_DIM
    dkv = N_KV_HEADS * HEAD_DIM

    def kern(qb_ref, kt_ref, case_ref, first_ref, last_ref, fetch_ref, *refs):
        _attn_kernel(qb_ref, kt_ref, case_ref, first_ref, last_ref, *refs, tq=tq, tk=tk)

    return pl.pallas_call(
        kern,
        grid_spec=pltpu.PrefetchScalarGridSpec(
            num_scalar_prefetch=6,
            grid=(bsz, len(qb)),
            in_specs=[pl.BlockSpec((1, tq, d), lambda b, s, qb, kt, cs, fi, la, fe: (b, qb[s], 0)),
                      pl.BlockSpec((1, tk, dkv), lambda b, s, qb, kt, cs, fi, la, fe: (b, kt[s], 0)),
                      pl.BlockSpec((1, tk, dkv), lambda b, s, qb, kt, cs, fi, la, fe: (b, kt[s], 0)),
                      pl.BlockSpec((1, tq, tk), lambda b, s, qb, kt, cs, fi, la, fe: (b, qb[s], kt[s])),
                      pl.BlockSpec((1, N_HEADS, tq, tk), lambda b, s, qb, kt, cs, fi, la, fe: (fe[s], 0, 0, 0))],
            out_specs=pl.BlockSpec((1, tq, d), lambda b, s, qb, kt, cs, fi, la, fe: (b, qb[s], 0)),
            scratch_shapes=[pltpu.VMEM((N_HEADS, tq, HEAD_DIM), F32),
                            pltpu.VMEM((N_HEADS, tq, LANES), F32),
                            pltpu.VMEM((N_HEADS, tq, LANES), F32),
                            pltpu.VMEM((GROUP * tq, tk), F32),
                            pltpu.VMEM((GROUP * tq, tk), BF16),
                            pltpu.VMEM((GROUP * tq, LANES), F32),
                            pltpu.VMEM((GROUP * tq, LANES), F32),
                            pltpu.VMEM((tq, tk), F32)],
        ),
        out_shape=jax.ShapeDtypeStruct((bsz, t, d), BF16),
        compiler_params=_cparams("parallel", "arbitrary"),
        name="attention",
    )(*tabs, q, k, v, mask, bias)


def _pad_rows(a, lp):
    return jnp.pad(a, ((0, 0), (0, lp - a.shape[1]), (0, 0)))


D_KV = N_KV_HEADS * HEAD_DIM
N_QI = N_IDX_HEADS * IDX_DIM
PAIR = 2 * IDX_DIM


def _attn_weight(wq, wk, wv, idx_wq, idx_wk, idx_ww):
    d = wq.shape[0]
    z = jnp.zeros((d, IDX_DIM), F32)
    zw = jnp.zeros((d, LANES - N_IDX_HEADS), F32)
    return jnp.concatenate([wq, wk, wv, idx_wq, idx_wk, z, z, idx_wk, idx_ww, zw], axis=1).astype(BF16)


def _dsa_layer(h, past, past_len, rel_bias, gain, w_att, wo):
    bsz, t, d = h.shape
    hf = h.reshape(bsz * t, d)
    c_k, c_v, c_qi = d, d + D_KV, d + 2 * D_KV
    c_ka = c_qi + N_QI
    c_kb, c_wi = c_ka + PAIR, c_ka + 2 * PAIR
    segs = ((0, d, None, HEAD_DIM ** -0.5 * LOG2E, ((0, d),)),
            (c_k, D_KV, None, 1.0, ((1, D_KV), (2, D_KV))),
            (c_v, D_KV, None, 1.0, ((3, D_KV), (4, D_KV))),
            (c_qi, N_QI, None, 1.0, ((5, N_QI),)),
            (c_ka, PAIR, None, 1.0, ((6, PAIR), (7, IDX_DIM))),
            (c_kb, PAIR, None, 1.0, ((8, PAIR),)),
            (c_wi, LANES, None, IDX_SCALE, ((9, N_IDX_HEADS),)))
    out_defs = ((d, BF16), (D_KV, F32), (D_KV, BF16), (D_KV, F32), (D_KV, BF16), (N_QI, BF16),
                (PAIR, BF16), (IDX_DIM, F32), (PAIR, BF16), (N_IDX_HEADS, F32))
    q, k, k16, v, v16, qi, ka, ki, kb, wi = [
        a.reshape(bsz, t, -1) for a in _rows_matmul(hf, w_att, segs, out_defs, gain=gain, name="attn_proj")]
    if past is not None:
        ki_past = past[2].astype(BF16)
        zeros = jnp.zeros_like(ki_past)
        k16 = jnp.concatenate([past[0].reshape(bsz, past_len, D_KV).astype(BF16), k16], axis=1)
        v16 = jnp.concatenate([past[1].reshape(bsz, past_len, D_KV).astype(BF16), v16], axis=1)
        ka = jnp.concatenate([jnp.concatenate([ki_past, zeros], axis=-1), ka], axis=1)
        kb = jnp.concatenate([jnp.concatenate([zeros, ki_past], axis=-1), kb], axis=1)
    l_valid = past_len + t
    n_keep = min(TOPK_MAX, l_valid // 4)
    tk = 512 if l_valid % 512 == 0 else 256
    lp = -(-l_valid // tk) * tk
    k16, v16, ka, kb = (_pad_rows(a, lp) for a in (k16, v16, ka, kb))
    mask = _index_mask(qi, wi, ka, kb, q_pos0=past_len, l_valid=l_valid, n_keep=n_keep)
    o = _attention(q, k16, v16, mask, rel_bias, q_pos0=past_len, l_valid=l_valid)
    hout, = _rows_matmul(o.reshape(bsz * t, d), wo, ((0, d, None, 1.0, ((0, d),)),), ((d, F32),),
                         res=hf, name="attn_out")
    return hout.reshape(bsz, t, d), k, v, ki


def _conv_layer(h, past, gain, w_pw1, b_pw1, w_dw, b_dw, ln_g, ln_b, w_pw2, b_pw2):
    bsz, t, d = h.shape
    hf = h.reshape(bsz * t, d)
    c = w_pw1.shape[1] // 2
    u, = _rows_matmul(hf, w_pw1, ((0, c, c, 1.0, ((0, c),)),), ((c, F32),), gain=gain, bias=b_pw1,
                      name="conv_glu")
    u = u.reshape(bsz, t, c)
    if past is None:
        past = jnp.zeros((bsz, CONV_STATE, c), F32)
    past32 = jnp.concatenate([jnp.zeros((bsz, HALO - CONV_STATE, c), F32), past], axis=1)
    z = _dwconv_ln_swish(u, past32, w_dw, b_dw, ln_g, ln_b)
    hout, = _rows_matmul(z.reshape(bsz * t, c), w_pw2, ((0, d, None, 1.0, ((0, d),)),), ((d, F32),),
                         bias=b_pw2, res=hf, name="conv_out")
    state = jnp.concatenate([past, u], axis=1)[:, -CONV_STATE:]
    return hout.reshape(bsz, t, d), state


def _trunk(x, past, past_len, p):
    bsz, t, d = x.shape
    h, k, v, ki = _dsa_layer(x, None if past is None else past[:3], past_len, p["rel_bias"],
                             p["norm_mix"][0], p["w_att"], p["wo"])
    h = _ffn(h.reshape(bsz * t, d), p["norm_ffn"][0], p["wg"][0], p["wu"][0], p["wd"][0],
             p["norm_final"], False).reshape(bsz, t, d)
    h, cs = _conv_layer(h, None if past is None else past[3], p["norm_mix"][1], p["w_pw1"], p["b_pw1"],
                        p["w_dw"], p["b_dw"], p["ln_g"], p["ln_b"], p["w_pw2"], p["b_pw2"])
    y = _ffn(h.reshape(bsz * t, d), p["norm_ffn"][1], p["wg"][1], p["wu"][1], p["wd"][1],
             p["norm_final"], True).reshape(bsz, t, d)
    return (y, k.reshape(1, bsz, t, N_KV_HEADS, HEAD_DIM), v.reshape(1, bsz, t, N_KV_HEADS, HEAD_DIM),
            ki[None], cs[None])


def kernel(x_prompt, x_sample, cache_k, cache_v, cache_kidx, state_conv, rel_bias, norm_mix, norm_ffn, norm_final, attn_wq, attn_wk, attn_wv, attn_wo, idx_wq, idx_wk, idx_ww, conv_w_pw1, conv_b_pw1, conv_w_dw, conv_b_dw, conv_ln_g, conv_ln_b, conv_w_pw2, conv_b_pw2, ffn_w_gate, ffn_w_up, ffn_w_down):
    p = dict(
        rel_bias=rel_bias, norm_mix=norm_mix, norm_ffn=norm_ffn, norm_final=norm_final,
        w_att=_attn_weight(attn_wq[0], attn_wk[0], attn_wv[0], idx_wq[0], idx_wk[0], idx_ww[0]),
        wo=attn_wo[0].astype(BF16),
        wg=ffn_w_gate.astype(BF16), wu=ffn_w_up.astype(BF16), wd=ffn_w_down.astype(BF16),
        w_pw1=conv_w_pw1[0].astype(BF16), b_pw1=conv_b_pw1[0], w_dw=conv_w_dw[0], b_dw=conv_b_dw[0],
        ln_g=conv_ln_g[0], ln_b=conv_ln_b[0], w_pw2=conv_w_pw2[0].astype(BF16), b_pw2=conv_b_pw2[0],
    )
    y_p, k_p, v_p, ki_p, cs_p = _trunk(x_prompt, None, 0, p)
    past = (cache_k[0], cache_v[0], cache_kidx[0], state_conv[0])
    y_s, k_s, v_s, ki_s, cs_s = _trunk(x_sample, past, cache_k.shape[2], p)
    return (y_p, y_s, k_p, v_p, ki_p, cs_p, k_s, v_s, ki_s, cs_s)
```

```python
import functools
import math

import numpy as np
import jax
import jax.numpy as jnp
from jax import lax
from jax.experimental import pallas as pl
from jax.experimental.pallas import tpu as pltpu

CHUNK = 64
N_HEADS = 16
N_KV_HEADS = 4
GROUP = N_HEADS // N_KV_HEADS
HEAD_DIM = 128
N_IDX_HEADS = 16
IDX_DIM = 64
IDX_SCALE = (N_IDX_HEADS ** -0.5) * (IDX_DIM ** -0.5)
TOPK_MAX = 256
N_BUCKETS = 32
DW_WIDTH = 31
CONV_STATE = DW_WIDTH - 1
RMS_EPS = 1e-6
LN_EPS = 1e-5

LANES = 128
SUBLANES = 8
HALO = 32
MASK_NEG = -1e30
LOG2E = math.log2(math.e)
INT_MIN = -(2 ** 31)
VMEM_LIMIT = 48 * 1024 * 1024
FFN_VMEM_LIMIT = 56 * 1024 * 1024

F32 = jnp.float32
BF16 = jnp.bfloat16


def _cparams(*sem):
    return pltpu.CompilerParams(dimension_semantics=sem, vmem_limit_bytes=VMEM_LIMIT)


def _pick(n, prefs):
    for p in prefs:
        if n % p == 0:
            return p
    return n


def _rms(x, g, eps):
    return x * lax.rsqrt(jnp.mean(x * x, axis=-1, keepdims=True) + eps) * g


def _sigmoid(x):
    return 1.0 / (1.0 + jnp.exp(-x))


def _rows_kernel(*refs, norm, has_bias, has_res, segs, rs, tn):
    refs = list(refs)
    x_ref = refs.pop(0)
    g_ref = refs.pop(0) if norm else None
    w_ref = refs.pop(0)
    b_ref = refs.pop(0) if has_bias else None
    r_ref = refs.pop(0) if has_res else None
    outs = refs
    tm = x_ref.shape[0]
    for r0 in range(0, tm, rs):
        xa = x_ref[r0:r0 + rs, :]
        hn = _rms(xa, g_ref[...], RMS_EPS).astype(BF16) if norm else xa
        for col0, width, gate0, scale, seg_outs in segs:
            for c in range(0, width, tn):
                cw = min(tn, width - c)
                y = jnp.dot(hn, w_ref[:, col0 + c:col0 + c + cw], preferred_element_type=F32)
                if has_bias:
                    y = y + b_ref[:, col0 + c:col0 + c + cw]
                if gate0 is not None:
                    gt = jnp.dot(hn, w_ref[:, gate0 + c:gate0 + c + cw], preferred_element_type=F32)
                    if has_bias:
                        gt = gt + b_ref[:, gate0 + c:gate0 + c + cw]
                    y = y * _sigmoid(gt)
                if scale != 1.0:
                    y = y * scale
                if has_res:
                    y = y + r_ref[r0:r0 + rs, c:c + cw]
                for oi, ow in seg_outs:
                    if c < ow:
                        n_c = min(cw, ow - c)
                        outs[oi][r0:r0 + rs, c:c + n_c] = y[:, :n_c].astype(outs[oi].dtype)


def _rows_matmul(x, w, segs, out_defs, *, gain=None, bias=None, res=None, name):
    n, k = x.shape
    cols = w.shape[1]
    tm = _pick(n, (512, 256, 128, 64, 32, 16, 8))
    rs = _pick(tm, (256, 128, 64, 32, 16, 8))
    once = dict(pipeline_mode=pl.Buffered(1))
    args, in_specs = [x], [pl.BlockSpec((tm, k), lambda i: (i, 0))]
    if gain is not None:
        args.append(gain.reshape(1, k))
        in_specs.append(pl.BlockSpec((1, k), lambda i: (0, 0), **once))
    args.append(w)
    in_specs.append(pl.BlockSpec((k, cols), lambda i: (0, 0), **once))
    if bias is not None:
        args.append(bias.reshape(1, cols))
        in_specs.append(pl.BlockSpec((1, cols), lambda i: (0, 0), **once))
    if res is not None:
        args.append(res)
        in_specs.append(pl.BlockSpec((tm, res.shape[1]), lambda i: (i, 0)))
    kern = functools.partial(_rows_kernel, norm=gain is not None, has_bias=bias is not None,
                             has_res=res is not None, segs=segs, rs=rs, tn=512)
    return pl.pallas_call(
        kern,
        grid=(n // tm,),
        in_specs=in_specs,
        out_specs=[pl.BlockSpec((tm, ow), lambda i: (i, 0)) for ow, _ in out_defs],
        out_shape=[jax.ShapeDtypeStruct((n, ow), dt) for ow, dt in out_defs],
        compiler_params=_cparams("parallel"),
        name=name,
    )(*args)


def _ffn_kernel(x_ref, g_ref, wg_ref, wu_ref, wd_ref, gf_ref, o_ref, hn_ref, *, final_norm):
    j = pl.program_id(1)

    @pl.when(j == 0)
    def _():
        hn_ref[...] = _rms(x_ref[...], g_ref[...], RMS_EPS).astype(BF16)
        o_ref[...] = jnp.zeros_like(o_ref)

    hn = hn_ref[...]
    a = jnp.dot(hn, wg_ref[...], preferred_element_type=F32)
    u = jnp.dot(hn, wu_ref[...], preferred_element_type=F32)
    act = (a * _sigmoid(a)) * u
    o_ref[...] += jnp.dot(act.astype(BF16), wd_ref[...], preferred_element_type=F32)

    @pl.when(j == pl.num_programs(1) - 1)
    def _():
        y = x_ref[...] + o_ref[...]
        if final_norm:
            y = _rms(y, gf_ref[...], RMS_EPS)
        o_ref[...] = y


def _ffn(x, gain, wg, wu, wd, gain_final, final_norm):
    n, d = x.shape
    f = wg.shape[1]
    tm = _pick(n, (1024, 512, 256, 128, 64, 32, 16, 8))
    tf = _pick(f, (512, 256, 128))
    once = dict(pipeline_mode=pl.Buffered(1))
    return pl.pallas_call(
        functools.partial(_ffn_kernel, final_norm=final_norm),
        grid=(n // tm, f // tf),
        in_specs=[pl.BlockSpec((tm, d), lambda i, j: (i, 0), **once),
                  pl.BlockSpec((1, d), lambda i, j: (0, 0), **once),
                  pl.BlockSpec((d, tf), lambda i, j: (0, j)),
                  pl.BlockSpec((d, tf), lambda i, j: (0, j)),
                  pl.BlockSpec((tf, d), lambda i, j: (j, 0)),
                  pl.BlockSpec((1, d), lambda i, j: (0, 0), **once)],
        out_specs=pl.BlockSpec((tm, d), lambda i, j: (i, 0)),
        out_shape=jax.ShapeDtypeStruct((n, d), F32),
        scratch_shapes=[pltpu.VMEM((tm, d), BF16)],
        compiler_params=pltpu.CompilerParams(dimension_semantics=("parallel", "arbitrary"),
                                             vmem_limit_bytes=FFN_VMEM_LIMIT),
        name="ffn",
    )(x, gain.reshape(1, d), wg, wu, wd, gain_final.reshape(1, d))


def _dwconv_kernel(u_ref, halo_ref, past_ref, w_ref, b_ref, lg_ref, lb_ref, o_ref, win_ref, dw_ref, sh_ref,
                   *, tt, c):
    halo = jnp.where(pl.program_id(1) == 0, past_ref[0], halo_ref[0])
    win_ref[0:HALO, :] = halo
    win_ref[HALO:HALO + tt, :] = u_ref[0]

    span = tt + HALO - SUBLANES

    def chunk(ci, carry):
        off = pl.multiple_of(ci * LANES, LANES)
        for s in range(1, SUBLANES):
            sh_ref[s - 1] = win_ref[pl.ds(s, span), pl.ds(off, LANES)]
        acc = jnp.zeros((tt, LANES), F32) + b_ref[:, pl.ds(off, LANES)]
        for k in range(DW_WIDTH):
            r = k + HALO - CONV_STATE
            s, base = r % SUBLANES, r - r % SUBLANES
            rows = win_ref[pl.ds(base, tt), pl.ds(off, LANES)] if s == 0 else sh_ref[s - 1, base:base + tt, :]
            acc = acc + w_ref[k:k + 1, pl.ds(off, LANES)] * rows
        dw_ref[:, pl.ds(off, LANES)] = acc
        return carry

    lax.fori_loop(0, c // LANES, chunk, 0)
    df = dw_ref[...]
    mu = jnp.mean(df, axis=-1, keepdims=True)
    var = jnp.mean(jnp.square(df - mu), axis=-1, keepdims=True)
    z = (df - mu) * lax.rsqrt(var + LN_EPS) * lg_ref[...] + lb_ref[...]
    o_ref[0] = (z * _sigmoid(z)).astype(BF16)


def _dwconv_ln_swish(u, past32, w_dw, b_dw, ln_g, ln_b):
    bsz, t, c = u.shape
    tt = _pick(t, (256, 128, 64, 32))
    w32 = jnp.concatenate([w_dw, jnp.zeros((HALO - DW_WIDTH, c), F32)], axis=0)
    r = tt // HALO
    return pl.pallas_call(
        functools.partial(_dwconv_kernel, tt=tt, c=c),
        grid=(bsz, t // tt),
        in_specs=[pl.BlockSpec((1, tt, c), lambda b, i: (b, i, 0)),
                  pl.BlockSpec((1, HALO, c), lambda b, i: (b, jnp.maximum(i * r - 1, 0), 0)),
                  pl.BlockSpec((1, HALO, c), lambda b, i: (b, 0, 0)),
                  pl.BlockSpec((HALO, c), lambda b, i: (0, 0)),
                  pl.BlockSpec((1, c), lambda b, i: (0, 0)),
                  pl.BlockSpec((1, c), lambda b, i: (0, 0)),
                  pl.BlockSpec((1, c), lambda b, i: (0, 0))],
        out_specs=pl.BlockSpec((1, tt, c), lambda b, i: (b, i, 0)),
        out_shape=jax.ShapeDtypeStruct((bsz, t, c), BF16),
        scratch_shapes=[pltpu.VMEM((tt + HALO, c), F32), pltpu.VMEM((tt, c), F32),
                        pltpu.VMEM((SUBLANES - 1, tt + HALO - SUBLANES, LANES), F32)],
        compiler_params=_cparams("parallel", "arbitrary"),
        name="dwconv_ln_swish",
    )(u, u, past32, w32, b_dw.reshape(1, c), ln_g.reshape(1, c), ln_b.reshape(1, c))


def _adm_end(qpos, l_valid):
    return jnp.minimum((qpos // CHUNK + 1) * CHUNK, l_valid)


def _flip_magnitude(bits):
    return bits ^ ((bits >> 31) & jnp.int32(0x7FFFFFFF))


def _sort_key(x):
    return _flip_magnitude(lax.bitcast_convert_type(x, jnp.int32))


def _key_value(key):
    return lax.bitcast_convert_type(_flip_magnitude(key), F32)


def _index_mask_kernel(qi_ref, wt_ref, ka_ref, kb_ref, o_ref, s_ref, g_ref, *,
                       tq, tk, cw, lp, q_pos0, l_valid, n_keep):
    i = pl.program_id(1)
    q0 = q_pos0 + i * tq
    n_kt = (_adm_end(q0 + tq - 1, l_valid) + tk - 1) // tk
    n_all = lp // tk
    col_end = _adm_end(q0 + lax.broadcasted_iota(jnp.int32, (1, tq), 1), l_valid)
    ksub = 32
    kst = 128

    def score_tile(kt, carry):
        for st in range(tk // kst):
            off = pl.multiple_of(kt * tk + st * kst, kst)
            ka = ka_ref[0, pl.ds(off, kst), :]
            kb = kb_ref[0, pl.ds(off, kst), :]
            acc = jnp.zeros((kst, tq), F32)
            for p in range(N_IDX_HEADS // 2):
                q2 = qi_ref[0, :, 2 * IDX_DIM * p:2 * IDX_DIM * (p + 1)]
                for half, kk in ((0, ka), (1, kb)):
                    h = 2 * p + half
                    d = lax.dot_general(kk, q2, (((1,), (1,)), ((), ())), preferred_element_type=F32)
                    acc = acc + wt_ref[0, h:h + 1, :] * jnp.maximum(d, 0.0)
            adm = off + lax.broadcasted_iota(jnp.int32, (kst, tq), 0) < col_end
            s_ref[pl.ds(off, kst), :] = jnp.where(adm, _sort_key(acc), jnp.int32(INT_MIN))
            gsl = slice(st * kst, (st + 1) * kst)
            g_ref[gsl, :] = jnp.maximum(g_ref[gsl, :], jnp.where(adm, acc, -jnp.inf))
        return carry

    g_ref[...] = jnp.full((tk, tq), -jnp.inf, F32)

    def pad_tile(kt, carry):
        s_ref[pl.ds(pl.multiple_of(kt * tk, tk), tk), :] = jnp.full((tk, tq), INT_MIN, jnp.int32)
        return carry

    n_ct = (n_kt * tk + cw - 1) // cw
    lax.fori_loop(0, n_kt, score_tile, 0)
    lax.fori_loop(n_kt, n_ct * (cw // tk), pad_tile, 0)

    def count(pred, args):
        sub = lax.broadcasted_iota(jnp.int32, (ksub, tq), 0)

        def body(ct, acc):
            for r in range(cw // ksub):
                off = pl.multiple_of(ct * cw + r * ksub, ksub)
                acc = acc + jnp.where(pred(s_ref[pl.ds(off, ksub), :], sub + off, *args), 1.0, 0.0)
            return acc
        acc = lax.fori_loop(0, n_ct, body, jnp.zeros((ksub, tq), F32))
        return jnp.sum(acc, axis=0, keepdims=True)

    kf = float(n_keep)
    probes_per_check = 2

    g = g_ref[...]
    k_lo = _sort_key(jnp.min(g, axis=0, keepdims=True))
    k_hi = _sort_key(jnp.max(g, axis=0, keepdims=True))
    x = k_lo ^ k_hi
    xf_bits = lax.bitcast_convert_type(jnp.maximum(x, 1).astype(F32), jnp.int32)
    b0 = jnp.where(x < 0, 31, jnp.minimum((xf_bits >> 23) - 127, 31))
    if tk < n_keep:
        b0 = jnp.full((1, tq), 31, jnp.int32)
    top = lax.shift_left(jnp.int32(1), b0)
    u0 = (k_lo ^ jnp.int32(INT_MIN)) & ~(top | (top - 1))

    def probe(step, u, settled, thr_ge):
        b = b0 - step
        cand = jnp.where(b >= 0, u | lax.shift_left(jnp.int32(1), jnp.maximum(b, 0)), u)
        ct = cand ^ jnp.int32(INT_MIN)
        c = count(lambda blk, pos, t: blk >= t, [ct])
        exact = jnp.logical_and(c == kf, settled == 0)
        return (jnp.where(c >= kf, cand, u), jnp.where(exact, 1, settled), jnp.where(exact, ct, thr_ge))

    def n_open(settled, step):
        return jnp.max(jnp.where(jnp.logical_and(settled == 0, b0 - step >= 0), 1.0, 0.0))

    def probe_group(state):
        g, u, settled, thr_ge, _ = state
        for j in range(probes_per_check):
            u, settled, thr_ge = probe(g * probes_per_check + j, u, settled, thr_ge)
        return g + 1, u, settled, thr_ge, n_open(settled, (g + 1) * probes_per_check)

    settled0 = jnp.where(col_end <= n_keep, 1, 0)
    state0 = (jnp.int32(0), u0, settled0,
              jnp.full((1, tq), INT_MIN + 1, jnp.int32), n_open(settled0, 0))
    _, u, settled, thr_ge, _ = lax.while_loop(
        lambda st: jnp.logical_and(st[4] > 0.0, st[0] < 32 // probes_per_check), probe_group, state0)

    tie_rows = settled == 0
    thr_t = u ^ jnp.int32(INT_MIN)
    nbits = max(1, int(lp).bit_length())

    def tie_search():
        need = kf - count(lambda blk, pos, t: blk > t, [thr_t])

        def pos_step(s, p):
            cand = p | lax.shift_left(jnp.int32(1), nbits - 1 - s)
            c = count(lambda blk, pos, t, cd: jnp.logical_and(blk == t, pos < cd), [thr_t, cand])
            return jnp.where(c <= need, cand, p)
        return lax.fori_loop(0, nbits, pos_step, jnp.zeros((1, tq), jnp.int32))

    any_tie = jnp.max(jnp.where(tie_rows, 1.0, 0.0)) > 0.0
    p_cut = lax.cond(any_tie, tie_search, lambda: jnp.zeros((1, tq), jnp.int32))
    p_cut = jnp.where(tie_rows, p_cut, 0)
    thr = jnp.where(tie_rows, thr_t, thr_ge - 1)

    def write_tile(kt, carry):
        for st in range(tk // kst):
            off = pl.multiple_of(kt * tk + st * kst, kst)
            blk = s_ref[pl.ds(off, kst), :]
            kpos = off + lax.broadcasted_iota(jnp.int32, (kst, tq), 0)
            keep = jnp.logical_or(blk > thr, jnp.logical_and(blk == thr, kpos < p_cut))
            o_ref[0, pl.ds(off, kst), :] = jnp.where(keep, 0.0, MASK_NEG).astype(BF16)
        return carry

    def blank_tile(kt, carry):
        off = pl.multiple_of(kt * tk, tk)
        o_ref[0, pl.ds(off, tk), :] = jnp.full((tk, tq), MASK_NEG, BF16)
        return carry

    lax.fori_loop(0, n_kt, write_tile, 0)
    lax.fori_loop(n_kt, n_all, blank_tile, 0)


def _index_mask(qi, wi, ka, kb, *, q_pos0, l_valid, n_keep):
    bsz, t, _ = qi.shape
    lp = ka.shape[1]
    tp = -(-t // LANES) * LANES
    if tp != t:
        qi = jnp.pad(qi, ((0, 0), (0, tp - t), (0, 0)))
        wi = jnp.pad(wi, ((0, 0), (0, tp - t), (0, 0)))
    wt = jnp.swapaxes(wi, 1, 2)
    tq = _pick(tp, (256, 128))
    tk = 256
    kern = functools.partial(_index_mask_kernel, tq=tq, tk=tk, cw=tk, lp=lp,
                             q_pos0=q_pos0, l_valid=l_valid, n_keep=n_keep)
    mask = pl.pallas_call(
        kern,
        grid=(bsz, tp // tq),
        in_specs=[pl.BlockSpec((1, tq, N_IDX_HEADS * IDX_DIM), lambda b, i: (b, i, 0)),
                  pl.BlockSpec((1, N_IDX_HEADS, tq), lambda b, i: (b, 0, i)),
                  pl.BlockSpec((1, lp, 2 * IDX_DIM), lambda b, i: (b, 0, 0)),
                  pl.BlockSpec((1, lp, 2 * IDX_DIM), lambda b, i: (b, 0, 0))],
        out_specs=pl.BlockSpec((1, lp, tq), lambda b, i: (b, 0, i)),
        out_shape=jax.ShapeDtypeStruct((bsz, lp, tp), BF16),
        scratch_shapes=[pltpu.VMEM((lp, tq), jnp.int32), pltpu.VMEM((tk, tq), F32)],
        compiler_params=_cparams("parallel", "arbitrary"),
        name="index_mask",
    )(qi, wt, ka, kb)
    return mask


_BUCKET_STARTS = tuple(int(math.ceil(8 * 16 ** ((b - 8) / 8) - 1e-9)) for b in range(9, 16))
FAR_BUCKET = N_BUCKETS // 2 - 1


def _bias_tile_kernel(delta_ref, tbl_ref, o_ref, *, tq, tk):
    case = pl.program_id(0)
    h = pl.program_id(1)
    rel = (delta_ref[case] + lax.broadcasted_iota(jnp.int32, (tq, tk), 1)
           - lax.broadcasted_iota(jnp.int32, (tq, tk), 0))
    n = jnp.abs(rel)
    large = jnp.full((tq, tk), 8, jnp.int32)
    for st in _BUCKET_STARTS:
        large = large + jnp.where(n >= st, 1, 0)
    bucket = jnp.where(rel > 0, N_BUCKETS // 2, 0) + jnp.where(n < 8, n, large)

    def body(b, acc):
        return jnp.where(bucket == b, tbl_ref[b * N_HEADS + h], acc)

    acc = lax.fori_loop(0, N_BUCKETS, body, jnp.zeros((tq, tk), F32))
    o_ref[0, 0] = (acc - tbl_ref[FAR_BUCKET * N_HEADS + h]) * LOG2E


def _bias_tiles(rel_bias, deltas, tq, tk):
    ncase = len(deltas)
    return pl.pallas_call(
        functools.partial(_bias_tile_kernel, tq=tq, tk=tk),
        grid_spec=pltpu.PrefetchScalarGridSpec(
            num_scalar_prefetch=2,
            grid=(ncase, N_HEADS),
            in_specs=[],
            out_specs=pl.BlockSpec((1, 1, tq, tk), lambda c, h, d, t: (c, h, 0, 0)),
        ),
        out_shape=jax.ShapeDtypeStruct((ncase, N_HEADS, tq, tk), F32),
        compiler_params=_cparams("arbitrary", "arbitrary"),
        name="bias_tiles",
    )(jnp.asarray(deltas, jnp.int32), rel_bias.reshape(-1))


def _attn_kernel(qb_ref, kt_ref, case_ref, first_ref, last_ref,
                 q_ref, k_ref, v_ref, mask_ref, bias_ref, eye_ref, o_ref, acc_ref, m_ref, l_ref,
                 s_ref, p_ref, a_ref, r_ref, *, tq, tk):
    s_id = pl.program_id(1)

    @pl.when(first_ref[s_id] == 1)
    def _():
        acc_ref[...] = jnp.zeros_like(acc_ref)
        m_ref[...] = jnp.full_like(m_ref, -jnp.inf)
        l_ref[...] = jnp.zeros_like(l_ref)

    rc = _pick(tq, (16, 8))
    rg = min(tq, LANES)

    def run(with_bias):
        for g in range(N_KV_HEADS):
            h0 = g * GROUP
            kg = k_ref[0, :, g * HEAD_DIM:(g + 1) * HEAD_DIM]
            vg = v_ref[0, :, g * HEAD_DIM:(g + 1) * HEAD_DIM]
            for hb in range(tq // rg):
                rows = slice(hb * rg, (hb + 1) * rg)
                qa = jnp.concatenate(
                    [jnp.concatenate([q_ref[0, rows, (h0 + j) * HEAD_DIM:(h0 + j + 1) * HEAD_DIM],
                                      eye_ref[rows, :]], axis=1) for j in range(GROUP)], axis=0)
                ka = jnp.concatenate([kg, mask_ref[0, :, hb * LANES:(hb + 1) * LANES]], axis=1)
                res = lax.dot_general(qa, ka, (((1,), (1,)), ((), ())), preferred_element_type=F32)
                for j in range(GROUP):
                    s_ref[j * tq + hb * rg:j * tq + (hb + 1) * rg, :] = res[j * rg:(j + 1) * rg]
            nt = tk // LANES

            def logits(c, j):
                r0 = j * tq + c * rc
                sc = s_ref[r0:r0 + rc, :]
                if with_bias:
                    sc = sc + bias_ref[0, h0 + j, c * rc:(c + 1) * rc, :]
                return r0, sc

            for c in range(tq // rc):
                for j in range(GROUP):
                    r0, sc = logits(c, j)
                    mx = sc[:, 0:LANES]
                    for t in range(1, nt):
                        mx = jnp.maximum(mx, sc[:, t * LANES:(t + 1) * LANES])
                    r_ref[r0:r0 + rc, :] = mx
            m_prev = m_ref[h0:h0 + GROUP].reshape(GROUP * tq, LANES)
            m_new = jnp.maximum(m_prev, jnp.max(r_ref[...], axis=-1, keepdims=True))
            a_ref[...] = jnp.exp2(m_prev - m_new)
            m_ref[h0:h0 + GROUP] = m_new.reshape(GROUP, tq, LANES)
            for c in range(tq // rc):
                for j in range(GROUP):
                    r0, sc = logits(c, j)
                    mb = m_ref[h0 + j, c * rc:(c + 1) * rc, :]
                    p = jnp.exp2(sc - jnp.concatenate([mb] * nt, axis=1))
                    ps = p[:, 0:LANES]
                    for t in range(1, nt):
                        ps = ps + p[:, t * LANES:(t + 1) * LANES]
                    r_ref[r0:r0 + rc, :] = ps
                    p_ref[r0:r0 + rc, :] = p.astype(BF16)
            alpha = a_ref[...].reshape(GROUP, tq, LANES)
            row_sum = jnp.sum(r_ref[...], axis=-1, keepdims=True).reshape(GROUP, tq, 1)
            l_ref[h0:h0 + GROUP] = alpha * l_ref[h0:h0 + GROUP] + row_sum
            pv = jnp.dot(p_ref[...], vg, preferred_element_type=F32)
            acc_ref[h0:h0 + GROUP] = alpha * acc_ref[h0:h0 + GROUP] + pv.reshape(GROUP, tq, HEAD_DIM)

    lax.cond(case_ref[s_id] >= 0, lambda: run(True), lambda: run(False))

    @pl.when(last_ref[s_id] == 1)
    def _():
        for h in range(N_HEADS):
            o_ref[0, :, h * HEAD_DIM:(h + 1) * HEAD_DIM] = (acc_ref[h] / l_ref[h]).astype(BF16)


def _attention(q, k, v, mask, rel_bias, *, q_pos0, l_valid):
    bsz, t, _ = q.shape
    lp = k.shape[1]
    tq = _pick(t, (256, 128, 64, 32, 16))
    tk = _pick(lp, (512, 256, 128))
    qb, kt, delta, first, last = [], [], [], [], []
    for i in range(t // tq):
        qlast = q_pos0 + i * tq + tq - 1
        n_kt = -(-min((qlast // CHUNK + 1) * CHUNK, l_valid) // tk)
        for j in range(n_kt):
            qb.append(i)
            kt.append(j)
            delta.append(j * tk - (q_pos0 + i * tq))
            first.append(int(j == 0))
            last.append(int(j == n_kt - 1))
    need = [d + tk - 1 > -LANES for d in delta]
    deltas = sorted({d for d, nd in zip(delta, need) if nd})
    case = [deltas.index(d) if nd else -1 for d, nd in zip(delta, need)]
    fetch, cur = [], 0
    for cs in case:
        cur = cs if cs >= 0 else cur
        fetch.append(cur)
    bias = _bias_tiles(rel_bias, deltas, tq, tk)
    tabs = [jnp.asarray(a, jnp.int32) for a in (qb, kt, case, first, last, fetch)]
    d = N_HEADS * HEAD_DIM
    dkv = N_KV_HEADS * HEAD_DIM
    lq = max(tq, LANES)
    assert mask.shape[2] % lq == 0 and (tq % LANES == 0 or t == tq)
    eye = (jnp.arange(tq)[:, None] % LANES == jnp.arange(LANES)[None, :]).astype(BF16)

    def kern(qb_ref, kt_ref, case_ref, first_ref, last_ref, fetch_ref, *refs):
        _attn_kernel(qb_ref, kt_ref, case_ref, first_ref, last_ref, *refs, tq=tq, tk=tk)

    return pl.pallas_call(
        kern,
        grid_spec=pltpu.PrefetchScalarGridSpec(
            num_scalar_prefetch=6,
            grid=(bsz, len(qb)),
            in_specs=[pl.BlockSpec((1, tq, d), lambda b, s, qb, kt, cs, fi, la, fe: (b, qb[s], 0)),
                      pl.BlockSpec((1, tk, dkv), lambda b, s, qb, kt, cs, fi, la, fe: (b, kt[s], 0)),
                      pl.BlockSpec((1, tk, dkv), lambda b, s, qb, kt, cs, fi, la, fe: (b, kt[s], 0)),
                      pl.BlockSpec((1, tk, lq), lambda b, s, qb, kt, cs, fi, la, fe: (b, kt[s], qb[s])),
                      pl.BlockSpec((1, N_HEADS, tq, tk), lambda b, s, qb, kt, cs, fi, la, fe: (fe[s], 0, 0, 0)),
                      pl.BlockSpec((tq, LANES), lambda b, s, qb, kt, cs, fi, la, fe: (0, 0))],
            out_specs=pl.BlockSpec((1, tq, d), lambda b, s, qb, kt, cs, fi, la, fe: (b, qb[s], 0)),
            scratch_shapes=[pltpu.VMEM((N_HEADS, tq, HEAD_DIM), F32),
                            pltpu.VMEM((N_HEADS, tq, LANES), F32),
                            pltpu.VMEM((N_HEADS, tq, LANES), F32),
                            pltpu.VMEM((GROUP * tq, tk), F32),
                            pltpu.VMEM((GROUP * tq, tk), BF16),
                            pltpu.VMEM((GROUP * tq, LANES), F32),
                            pltpu.VMEM((GROUP * tq, LANES), F32)],
        ),
        out_shape=jax.ShapeDtypeStruct((bsz, t, d), BF16),
        compiler_params=_cparams("parallel", "arbitrary"),
        name="attention",
    )(*tabs, q, k, v, mask, bias, eye)


def _pad_rows(a, lp):
    return jnp.pad(a, ((0, 0), (0, lp - a.shape[1]), (0, 0)))


D_KV = N_KV_HEADS * HEAD_DIM
N_QI = N_IDX_HEADS * IDX_DIM
PAIR = 2 * IDX_DIM


def _attn_weight(wq, wk, wv, idx_wq, idx_wk, idx_ww):
    d = wq.shape[0]
    z = jnp.zeros((d, IDX_DIM), F32)
    zw = jnp.zeros((d, LANES - N_IDX_HEADS), F32)
    return jnp.concatenate([wq, wk, wv, idx_wq, idx_wk, z, z, idx_wk, idx_ww, zw], axis=1).astype(BF16)


def _dsa_layer(h, past, past_len, rel_bias, gain, w_att, wo):
    bsz, t, d = h.shape
    hf = h.reshape(bsz * t, d)
    c_k, c_v, c_qi = d, d + D_KV, d + 2 * D_KV
    c_ka = c_qi + N_QI
    c_kb, c_wi = c_ka + PAIR, c_ka + 2 * PAIR
    segs = ((0, d, None, HEAD_DIM ** -0.5 * LOG2E, ((0, d),)),
            (c_k, D_KV, None, 1.0, ((1, D_KV), (2, D_KV))),
            (c_v, D_KV, None, 1.0, ((3, D_KV), (4, D_KV))),
            (c_qi, N_QI, None, 1.0, ((5, N_QI),)),
            (c_ka, PAIR, None, 1.0, ((6, PAIR), (7, IDX_DIM))),
            (c_kb, PAIR, None, 1.0, ((8, PAIR),)),
            (c_wi, LANES, None, IDX_SCALE, ((9, N_IDX_HEADS),)))
    out_defs = ((d, BF16), (D_KV, F32), (D_KV, BF16), (D_KV, F32), (D_KV, BF16), (N_QI, BF16),
                (PAIR, BF16), (IDX_DIM, F32), (PAIR, BF16), (N_IDX_HEADS, F32))
    q, k, k16, v, v16, qi, ka, ki, kb, wi = [
        a.reshape(bsz, t, -1) for a in _rows_matmul(hf, w_att, segs, out_defs, gain=gain, name="attn_proj")]
    if past is not None:
        ki_past = past[2].astype(BF16)
        zeros = jnp.zeros_like(ki_past)
        k16 = jnp.concatenate([past[0].reshape(bsz, past_len, D_KV).astype(BF16), k16], axis=1)
        v16 = jnp.concatenate([past[1].reshape(bsz, past_len, D_KV).astype(BF16), v16], axis=1)
        ka = jnp.concatenate([jnp.concatenate([ki_past, zeros], axis=-1), ka], axis=1)
        kb = jnp.concatenate([jnp.concatenate([zeros, ki_past], axis=-1), kb], axis=1)
    l_valid = past_len + t
    n_keep = min(TOPK_MAX, l_valid // 4)
    tk = 512 if l_valid % 512 == 0 else 256
    lp = -(-l_valid // tk) * tk
    k16, v16, ka, kb = (_pad_rows(a, lp) for a in (k16, v16, ka, kb))
    mask = _index_mask(qi, wi, ka, kb, q_pos0=past_len, l_valid=l_valid, n_keep=n_keep)
    o = _attention(q, k16, v16, mask, rel_bias, q_pos0=past_len, l_valid=l_valid)
    hout, = _rows_matmul(o.reshape(bsz * t, d), wo, ((0, d, None, 1.0, ((0, d),)),), ((d, F32),),
                         res=hf, name="attn_out")
    return hout.reshape(bsz, t, d), k, v, ki


def _conv_layer(h, past, gain, w_pw1, b_pw1, w_dw, b_dw, ln_g, ln_b, w_pw2, b_pw2):
    bsz, t, d = h.shape
    hf = h.reshape(bsz * t, d)
    c = w_pw1.shape[1] // 2
    u, = _rows_matmul(hf, w_pw1, ((0, c, c, 1.0, ((0, c),)),), ((c, F32),), gain=gain, bias=b_pw1,
                      name="conv_glu")
    u = u.reshape(bsz, t, c)
    if past is None:
        past = jnp.zeros((bsz, CONV_STATE, c), F32)
    past32 = jnp.concatenate([jnp.zeros((bsz, HALO - CONV_STATE, c), F32), past], axis=1)
    z = _dwconv_ln_swish(u, past32, w_dw, b_dw, ln_g, ln_b)
    hout, = _rows_matmul(z.reshape(bsz * t, c), w_pw2, ((0, d, None, 1.0, ((0, d),)),), ((d, F32),),
                         bias=b_pw2, res=hf, name="conv_out")
    state = jnp.concatenate([past, u], axis=1)[:, -CONV_STATE:]
    return hout.reshape(bsz, t, d), state


def _trunk(x, past, past_len, p):
    bsz, t, d = x.shape
    h, k, v, ki = _dsa_layer(x, None if past is None else past[:3], past_len, p["rel_bias"],
                             p["norm_mix"][0], p["w_att"], p["wo"])
    h = _ffn(h.reshape(bsz * t, d), p["norm_ffn"][0], p["wg"][0], p["wu"][0], p["wd"][0],
             p["norm_final"], False).reshape(bsz, t, d)
    h, cs = _conv_layer(h, None if past is None else past[3], p["norm_mix"][1], p["w_pw1"], p["b_pw1"],
                        p["w_dw"], p["b_dw"], p["ln_g"], p["ln_b"], p["w_pw2"], p["b_pw2"])
    y = _ffn(h.reshape(bsz * t, d), p["norm_ffn"][1], p["wg"][1], p["wu"][1], p["wd"][1],
             p["norm_final"], True).reshape(bsz, t, d)
    return (y, k.reshape(1, bsz, t, N_KV_HEADS, HEAD_DIM), v.reshape(1, bsz, t, N_KV_HEADS, HEAD_DIM),
            ki[None], cs[None])


def kernel(x_prompt, x_sample, cache_k, cache_v, cache_kidx, state_conv, rel_bias, norm_mix, norm_ffn, norm_final, attn_wq, attn_wk, attn_wv, attn_wo, idx_wq, idx_wk, idx_ww, conv_w_pw1, conv_b_pw1, conv_w_dw, conv_b_dw, conv_ln_g, conv_ln_b, conv_w_pw2, conv_b_pw2, ffn_w_gate, ffn_w_up, ffn_w_down):
    p = dict(
        rel_bias=rel_bias, norm_mix=norm_mix, norm_ffn=norm_ffn, norm_final=norm_final,
        w_att=_attn_weight(attn_wq[0], attn_wk[0], attn_wv[0], idx_wq[0], idx_wk[0], idx_ww[0]),
        wo=attn_wo[0].astype(BF16),
        wg=ffn_w_gate.astype(BF16), wu=ffn_w_up.astype(BF16), wd=ffn_w_down.astype(BF16),
        w_pw1=conv_w_pw1[0].astype(BF16), b_pw1=conv_b_pw1[0], w_dw=conv_w_dw[0], b_dw=conv_b_dw[0],
        ln_g=conv_ln_g[0], ln_b=conv_ln_b[0], w_pw2=conv_w_pw2[0].astype(BF16), b_pw2=conv_b_pw2[0],
    )
    y_p, k_p, v_p, ki_p, cs_p = _trunk(x_prompt, None, 0, p)
    past = (cache_k[0], cache_v[0], cache_kidx[0], state_conv[0])
    y_s, k_s, v_s, ki_s, cs_s = _trunk(x_sample, past, cache_k.shape[2], p)
    return (y_p, y_s, k_p, v_p, ki_p, cs_p, k_s, v_s, ki_s, cs_s)
```
